```python
import jax, jax.numpy as jnp
from jax import lax
import numpy as np

D_MODEL = 1024
BATCH = 8
SEQ = 2048
DEPTH = 2
DEC_BATCH = 128
DEC_SEQ = 1
PAST_LEN = 16384
PAGE_SIZE = 128

D_CONV = D_MODEL // 2
D_POOL = D_MODEL - D_CONV
CONV_W = 3
POOL_WINDOWS = (2, 4, 8, 16)
N_POOL_GROUPS = len(POOL_WINDOWS)
POOL_GROUP = D_POOL // N_POOL_GROUPS
POOL_MAX = max(POOL_WINDOWS)
D_IN = 3 * D_CONV + D_POOL
PEER_HEADS = 8
N_KEYS = 128
N_EXPERTS = N_KEYS * N_KEYS
PEER_TOPK = 16
D_KEY = 256
D_HALF = D_KEY // 2
PEER_BLOCK = 256
ALPHA = (2 * DEPTH) ** 0.25
BETA = (8 * DEPTH) ** -0.25
LN_EPS = 1e-5

kernel_name = 'hymba_conv_pool_peer_decoder_step'


def _ln_stats(x):
    xf = x.astype(jnp.float32)
    mu = jnp.mean(xf, axis=-1, keepdims=True)
    var = jnp.mean(jnp.square(xf - mu), axis=-1, keepdims=True)
    return (xf - mu) * lax.rsqrt(var + LN_EPS)


def layer_norm(x, g, b):
    return (_ln_stats(x) * g + b).astype(x.dtype)


def modulate(x, shift, scale):
    return (_ln_stats(x) * (1.0 + scale[:, None, :]) + shift[:, None, :]).astype(x.dtype)


def short_conv_group(bg, cg, hc, prefix, conv_w):
    T = hc.shape[1]
    u = cg * hc
    u_ext = jnp.concatenate([prefix.astype(u.dtype), u], axis=1)
    y = u_ext[:, 0:T] * conv_w[0]
    for k in range(1, CONV_W):
        y = y + u_ext[:, k:k + T] * conv_w[k]
    return bg * y, u_ext[:, -(CONV_W - 1):]


def pool_group(hp, prefix, start_pos, pool_w, pool_scale):
    Bsz, T, _ = hp.shape
    P = POOL_MAX - 1
    p_ext = jnp.concatenate([prefix.astype(hp.dtype), hp], axis=1)
    cs = jnp.cumsum(p_ext.astype(jnp.float32), axis=1)
    cs = jnp.pad(cs, ((0, 0), (1, 0), (0, 0)))
    pos = jnp.arange(T, dtype=jnp.float32) + start_pos
    hpf = hp.astype(jnp.float32)
    outs = []
    for g, w in enumerate(POOL_WINDOWS):
        sl = slice(g * POOL_GROUP, (g + 1) * POOL_GROUP)
        win_sum = cs[:, P + 1:P + 1 + T, sl] - cs[:, P + 1 - w:P + 1 - w + T, sl]
        cnt = jnp.minimum(pos + 1.0, float(w))[None, :, None]
        outs.append(win_sum / cnt - hpf[:, :, sl])
    pooled = jnp.stack(outs, axis=2).astype(hp.dtype)
    mixed = jnp.einsum('btgc,gcd->btgd', pooled, pool_w).reshape(Bsz, T, D_POOL)
    return mixed * pool_scale, p_ext[:, -P:]


def peer_ffn(h, w_q, sub_keys, u_tab, v_tab):
    Bsz, T, D = h.shape
    flat = h.reshape(Bsz * T, D)
    n = flat.shape[0]
    pad = (-n) % PEER_BLOCK
    blocks = jnp.pad(flat, ((0, pad), (0, 0))).reshape(-1, PEER_BLOCK, D)

    def one_block(hb):
        q = (hb @ w_q).reshape(PEER_BLOCK, PEER_HEADS, 2, D_HALF)
        s = jnp.einsum('thpc,hpkc->thpk', q, sub_keys).astype(jnp.float32)
        s1, i1 = lax.top_k(s[:, :, 0], PEER_TOPK)
        s2, i2 = lax.top_k(s[:, :, 1], PEER_TOPK)
        cand = (s1[..., :, None] + s2[..., None, :]).reshape(PEER_BLOCK, PEER_HEADS, PEER_TOPK * PEER_TOPK)
        top_s, top_i = lax.top_k(cand, PEER_TOPK)
        e1 = jnp.take_along_axis(i1, top_i // PEER_TOPK, axis=-1)
        e2 = jnp.take_along_axis(i2, top_i % PEER_TOPK, axis=-1)
        experts = e1 * N_KEYS + e2
        gates = jax.nn.softmax(top_s, axis=-1)
        u = u_tab[experts]
        act = jnp.einsum('thkd,td->thk', u, hb).astype(jnp.float32)
        coef = (gates * jax.nn.gelu(act, approximate=False)).astype(hb.dtype)
        return jnp.einsum('thk,thkd->td', coef, v_tab[experts])

    out = lax.map(one_block, blocks).reshape(-1, D)[:n]
    return out.reshape(Bsz, T, D)


def run_group(x, c, conv_prefix, pool_prefix, start_pos,
              w_ada, b_ada, w_in, conv_w, pool_w, pool_scale, w_out, ln1_g, ln1_b,
              w_q, sub_keys, u_experts, v_experts, ln2_g, ln2_b):
    conv_states, pool_states = [], []
    for l in range(DEPTH):
        mod = jax.nn.silu(c) @ w_ada[l] + b_ada[l]
        sh1, sc1, g1, sh2, sc2, g2 = jnp.split(mod, 6, axis=-1)
        h = modulate(x, sh1, sc1)
        z = h @ w_in[l]
        bg, cg, hc, hp = jnp.split(z, [D_CONV, 2 * D_CONV, 3 * D_CONV], axis=-1)
        conv_out, conv_st = short_conv_group(bg, cg, hc, conv_prefix[l], conv_w[l])
        pool_out, pool_st = pool_group(hp, pool_prefix[l], start_pos, pool_w[l], pool_scale[l])
        mix = jnp.concatenate([conv_out, pool_out], axis=-1) @ w_out[l]
        x = layer_norm(ALPHA * x + g1[:, None, :] * mix, ln1_g[l], ln1_b[l])
        h2 = modulate(x, sh2, sc2)
        f = peer_ffn(h2, w_q[l], sub_keys[l], u_experts[l], v_experts[l])
        x = layer_norm(ALPHA * x + g2[:, None, :] * f, ln2_g[l], ln2_b[l])
        conv_states.append(conv_st)
        pool_states.append(pool_st)
    return x, jnp.stack(conv_states), jnp.stack(pool_states)


def setup_inputs(seed: int = 0) -> dict:
    key = jax.random.key(seed)
    ks = jax.random.split(key, 24)
    nrm = jax.random.normal
    f32 = jnp.float32
    D = D_MODEL
    return {
        'x_prompt': nrm(ks[0], (BATCH, SEQ, D), f32),
        'x_sample': nrm(ks[1], (DEC_BATCH, DEC_SEQ, D), f32),
        'state_conv': nrm(ks[2], (DEPTH, DEC_BATCH, CONV_W - 1, D_CONV), f32),
        'state_pool': nrm(ks[3], (DEPTH, DEC_BATCH, POOL_MAX - 1, D_POOL), f32),
        'c_prompt': nrm(ks[4], (BATCH, D), f32),
        'c_sample': nrm(ks[5], (DEC_BATCH, D), f32),
        'w_ada': nrm(ks[6], (DEPTH, D, 6 * D), f32) * (0.5 * D ** -0.5),
        'b_ada': nrm(ks[7], (DEPTH, 6 * D), f32) * 0.02,
        'w_in': nrm(ks[8], (DEPTH, D, D_IN), f32) * D ** -0.5,
        'conv_w': nrm(ks[9], (DEPTH, CONV_W, D_CONV), f32) * CONV_W ** -0.5,
        'pool_w': nrm(ks[10], (DEPTH, N_POOL_GROUPS, POOL_GROUP, POOL_GROUP), f32) * POOL_GROUP ** -0.5,
        'pool_scale': 1.0 + 0.1 * nrm(ks[11], (DEPTH, D_POOL), f32),
        'w_out': nrm(ks[12], (DEPTH, D, D), f32) * (BETA * D ** -0.5),
        'ln1_g': 1.0 + 0.02 * nrm(ks[13], (DEPTH, D), f32),
        'ln1_b': 0.02 * nrm(ks[14], (DEPTH, D), f32),
        'w_q': nrm(ks[15], (DEPTH, D, PEER_HEADS * D_KEY), f32) * D ** -0.5,
        'sub_keys': nrm(ks[16], (DEPTH, PEER_HEADS, 2, N_KEYS, D_HALF), f32) * D_HALF ** -0.5,
        'u_experts': nrm(ks[17], (DEPTH, N_EXPERTS, D), f32) * D ** -0.5,
        'v_experts': nrm(ks[18], (DEPTH, N_EXPERTS, D), f32) * BETA,
        'ln2_g': 1.0 + 0.02 * nrm(ks[19], (DEPTH, D), f32),
        'ln2_b': 0.02 * nrm(ks[20], (DEPTH, D), f32),
    }


def reference(x_prompt, x_sample, state_conv, state_pool, c_prompt, c_sample,
              w_ada, b_ada, w_in, conv_w, pool_w, pool_scale, w_out, ln1_g, ln1_b,
              w_q, sub_keys, u_experts, v_experts, ln2_g, ln2_b):
    weights = (w_ada, b_ada, w_in, conv_w, pool_w, pool_scale, w_out, ln1_g, ln1_b,
               w_q, sub_keys, u_experts, v_experts, ln2_g, ln2_b)
    n_prompt = x_prompt.shape[0]
    zero_conv = jnp.zeros((DEPTH, n_prompt, CONV_W - 1, D_CONV), x_prompt.dtype)
    zero_pool = jnp.zeros((DEPTH, n_prompt, POOL_MAX - 1, D_POOL), x_prompt.dtype)
    y_prompt, conv_prompt, pool_prompt = run_group(
        x_prompt, c_prompt, zero_conv, zero_pool, 0, *weights)
    y_sample, conv_sample, pool_sample = run_group(
        x_sample, c_sample, state_conv, state_pool, PAST_LEN, *weights)
    return (y_prompt, y_sample, conv_prompt, pool_prompt, conv_sample, pool_sample)
```

```python
import functools
import math

import jax
import jax.numpy as jnp
import numpy as np
from jax import lax
from jax.experimental import pallas as pl
from jax.experimental.pallas import tpu as pltpu

F32 = jnp.float32
BF16 = jnp.bfloat16

D_MODEL = 1024
DEPTH = 2
D_CONV = D_MODEL // 2
D_POOL = D_MODEL - D_CONV
CONV_W = 3
POOL_WINDOWS = (2, 4, 8, 16)
POOL_GROUP = D_POOL // len(POOL_WINDOWS)
POOL_MAX = max(POOL_WINDOWS)
D_IN = 3 * D_CONV + D_POOL
PEER_HEADS = 8
N_KEYS = 128
N_EXPERTS = N_KEYS * N_KEYS
PEER_TOPK = 16
D_KEY = 256
D_HALF = D_KEY // 2
PAST_LEN = 16384
ALPHA = (2 * DEPTH) ** 0.25
LN_EPS = 1e-5

LANES = 128
SUBLANES = 8
BF16_ROWS = 16
VMEM_LIMIT_BYTES = 56 * 1024 * 1024

MIX_TILE = 512
PEER_TILE = 512
EXPERT_CHUNK = 1024
KEYS_PER_CHUNK = EXPERT_CHUNK // N_KEYS
ADA_COLS = 1536

_CAND_SEGMENTS = ((0, 16), (1, 8), (2, 8), (3, 8), (4, 4), (5, 4), (6, 4), (7, 4)) + tuple(
    (r, 1) for r in range(8, 16))
N_CAND = sum(n for _, n in _CAND_SEGMENTS)
assert N_CAND == 64


def _ln_stats(x):
    mu = jnp.mean(x, axis=-1, keepdims=True)
    xc = x - mu
    var = jnp.mean(xc * xc, axis=-1, keepdims=True)
    return xc * lax.rsqrt(var + LN_EPS)


def _gelu(x):
    return 0.5 * x * (1.0 + lax.erf(x * (1.0 / math.sqrt(2.0))))


def _dot(a, b):
    return jnp.dot(a, b, preferred_element_type=F32)


def _ada_kernel(c_ref, w_ref, b_ref, o_ref):
    c = c_ref[...]
    act = (c * jax.nn.sigmoid(c)).astype(BF16)
    o_ref[0] = _dot(act, w_ref[0].astype(BF16)) + b_ref[0]


def _ada_modulation(c_all, w_ada, b_ada):
    n = c_all.shape[0]
    n_cols = 6 * D_MODEL
    return pl.pallas_call(
        _ada_kernel,
        grid=(DEPTH, n_cols // ADA_COLS),
        in_specs=[
            pl.BlockSpec((n, D_MODEL), lambda l, j: (0, 0)),
            pl.BlockSpec((1, D_MODEL, ADA_COLS), lambda l, j: (l, 0, j)),
            pl.BlockSpec((1, 1, ADA_COLS), lambda l, j: (l, 0, j)),
        ],
        out_specs=pl.BlockSpec((1, n, ADA_COLS), lambda l, j: (l, 0, j)),
        out_shape=jax.ShapeDtypeStruct((DEPTH, n, n_cols), F32),
        compiler_params=pltpu.CompilerParams(
            dimension_semantics=("arbitrary", "arbitrary"),
            vmem_limit_bytes=VMEM_LIMIT_BYTES),
        name="ada_modulation",
    )(c_all, w_ada, b_ada.reshape(DEPTH, 1, n_cols))


def _mix_tail(x, g1, conv_out, pooled, poolw_ref, pscale, wout_ref, lng, lnb):
    mixed = [
        _dot(pooled[g].astype(BF16), poolw_ref[g]) for g in range(len(POOL_WINDOWS))
    ]
    pool_out = jnp.concatenate(mixed, axis=-1) * pscale
    cat = jnp.concatenate([conv_out, pool_out], axis=-1).astype(BF16)
    mix = _dot(cat, wout_ref[...])
    return _ln_stats(ALPHA * x + g1 * mix) * lng + lnb


def _mix_prompt_kernel(x_ref, sh_ref, sc_ref, g_ref, win_ref, cw_ref, poolw_ref,
                       pscale_ref, wout_ref, lng_ref, lnb_ref,
                       o_ref, cst_ref, pst_ref, ue_scr, pe_scr):
    j = pl.program_id(1)
    nt = pl.num_programs(1)
    T = MIX_TILE
    HU = SUBLANES
    HP = POOL_MAX

    @pl.when(j == 0)
    def _():
        ue_scr[0:HU, :] = jnp.zeros((HU, D_CONV), F32)
        pe_scr[0:HP, :] = jnp.zeros((HP, D_POOL), F32)

    x = x_ref[0]
    h = _ln_stats(x) * (1.0 + sc_ref[0]) + sh_ref[0]
    z = _dot(h.astype(BF16), win_ref[...])
    bg = z[:, 0:D_CONV]
    cg = z[:, D_CONV:2 * D_CONV]
    hc = z[:, 2 * D_CONV:3 * D_CONV]
    hp = z[:, 3 * D_CONV:]

    u = cg * hc
    ue_scr[HU:HU + T, :] = u
    pe_scr[HP:HP + T, :] = hp

    cw = cw_ref[...]
    y = (ue_scr[HU - 2:HU - 2 + T, :] * cw[0:1, :]
         + ue_scr[HU - 1:HU - 1 + T, :] * cw[1:2, :]
         + u * cw[2:3, :])
    conv_out = bg * y

    pos = (lax.broadcasted_iota(jnp.int32, (T, POOL_GROUP), 0) + j * T).astype(F32)
    pooled = []
    for g, w in enumerate(POOL_WINDOWS):
        lanes = slice(g * POOL_GROUP, (g + 1) * POOL_GROUP)
        win = hp[:, lanes]
        for k in range(1, w):
            win = win + pe_scr[HP - k:HP - k + T, lanes]
        cnt = jnp.minimum(pos + 1.0, float(w))
        pooled.append(win / cnt - hp[:, lanes])

    o_ref[0] = _mix_tail(x, g_ref[0], conv_out, pooled, poolw_ref, pscale_ref[...],
                         wout_ref, lng_ref[...], lnb_ref[...])

    @pl.when(j == nt - 1)
    def _():
        cst_ref[0] = ue_scr[HU + T - (CONV_W - 1):HU + T, :]
        pst_ref[0] = pe_scr[HP + T - (POOL_MAX - 1):HP + T, :]

    @pl.when(j < nt - 1)
    def _():
        ue_scr[0:HU, :] = ue_scr[T:T + HU, :]
        pe_scr[0:HP, :] = pe_scr[T:T + HP, :]


def _mix_prompt(x, sh, sc, g, win, cw, poolw, pscale, wout, lng, lnb):
    B, S, D = x.shape
    nt = S // MIX_TILE
    row = lambda b, j: (b, 0, 0)
    full2 = lambda b, j: (0, 0)
    full3 = lambda b, j: (0, 0, 0)
    return pl.pallas_call(
        _mix_prompt_kernel,
        grid=(B, nt),
        in_specs=[
            pl.BlockSpec((1, MIX_TILE, D), lambda b, j: (b, j, 0)),
            pl.BlockSpec((1, 1, D), row),
            pl.BlockSpec((1, 1, D), row),
            pl.BlockSpec((1, 1, D), row),
            pl.BlockSpec((D, D_IN), full2),
            pl.BlockSpec((CONV_W, D_CONV), full2),
            pl.BlockSpec((len(POOL_WINDOWS), POOL_GROUP, POOL_GROUP), full3),
            pl.BlockSpec((1, D_POOL), full2),
            pl.BlockSpec((D, D), full2),
            pl.BlockSpec((1, D), full2),
            pl.BlockSpec((1, D), full2),
        ],
        out_specs=[
            pl.BlockSpec((1, MIX_TILE, D), lambda b, j: (b, j, 0)),
            pl.BlockSpec((1, CONV_W - 1, D_CONV), row),
            pl.BlockSpec((1, POOL_MAX - 1, D_POOL), row),
        ],
        out_shape=[
            jax.ShapeDtypeStruct((B, S, D), F32),
            jax.ShapeDtypeStruct((B, CONV_W - 1, D_CONV), F32),
            jax.ShapeDtypeStruct((B, POOL_MAX - 1, D_POOL), F32),
        ],
        scratch_shapes=[
            pltpu.VMEM((MIX_TILE + SUBLANES, D_CONV), F32),
            pltpu.VMEM((MIX_TILE + POOL_MAX, D_POOL), F32),
        ],
        compiler_params=pltpu.CompilerParams(
            dimension_semantics=("arbitrary", "arbitrary"),
            vmem_limit_bytes=VMEM_LIMIT_BYTES),
        name="mix_prompt",
    )(x, sh, sc, g, win, cw, poolw, pscale, wout, lng, lnb)


def _mix_sample_kernel(x_ref, sh_ref, sc_ref, g_ref, cst_in_ref, pst_in_ref, win_ref,
                       cw_ref, poolw_ref, pscale_ref, wout_ref, lng_ref, lnb_ref,
                       o_ref, cst_ref, pst_ref):
    x = x_ref[...]
    h = _ln_stats(x) * (1.0 + sc_ref[...]) + sh_ref[...]
    z = _dot(h.astype(BF16), win_ref[...])
    bg = z[:, 0:D_CONV]
    cg = z[:, D_CONV:2 * D_CONV]
    hc = z[:, 2 * D_CONV:3 * D_CONV]
    hp = z[:, 3 * D_CONV:]

    u = cg * hc
    cw = cw_ref[...]
    prev2 = cst_in_ref[:, 0:D_CONV]
    prev1 = cst_in_ref[:, D_CONV:2 * D_CONV]
    conv_out = bg * (prev2 * cw[0:1, :] + prev1 * cw[1:2, :] + u * cw[2:3, :])
    cst_ref[:, 0:D_CONV] = prev1
    cst_ref[:, D_CONV:2 * D_CONV] = u

    n_hist = POOL_MAX - 1
    pooled = []
    for g, w in enumerate(POOL_WINDOWS):
        win = hp[:, g * POOL_GROUP:(g + 1) * POOL_GROUP]
        for k in range(1, w):
            lo = (n_hist - k) * D_POOL + g * POOL_GROUP
            win = win + pst_in_ref[:, lo:lo + POOL_GROUP]
        cnt = min(float(PAST_LEN) + 1.0, float(w))
        pooled.append(win / cnt - hp[:, g * POOL_GROUP:(g + 1) * POOL_GROUP])
    pst_ref[:, 0:(n_hist - 1) * D_POOL] = pst_in_ref[:, D_POOL:n_hist * D_POOL]
    pst_ref[:, (n_hist - 1) * D_POOL:n_hist * D_POOL] = hp

    o_ref[...] = _mix_tail(x, g_ref[...], conv_out, pooled, poolw_ref, pscale_ref[...],
                           wout_ref, lng_ref[...], lnb_ref[...])


def _mix_sample(x, sh, sc, g, cst, pst, win, cw, poolw, pscale, wout, lng, lnb):
    B, D = x.shape
    return pl.pallas_call(
        _mix_sample_kernel,
        out_shape=[
            jax.ShapeDtypeStruct((B, D), F32),
            jax.ShapeDtypeStruct(cst.shape, F32),
            jax.ShapeDtypeStruct(pst.shape, F32),
        ],
        compiler_params=pltpu.CompilerParams(vmem_limit_bytes=VMEM_LIMIT_BYTES),
        name="mix_sample",
    )(x, sh, sc, g, cst, pst, win, cw, poolw, pscale, wout, lng, lnb)


def _extract_top(s, idx, n_out):
    big = float(4 * N_EXPERTS)
    rank = jnp.full(s.shape, float(n_out), F32)
    vals = []
    for step in range(n_out):
        m = jnp.max(s, axis=0, keepdims=True)
        first = jnp.min(jnp.where(s == m, idx, big), axis=0, keepdims=True)
        sel = idx == first
        rank = jnp.where(sel, float(step), rank)
        s = jnp.where(sel, -jnp.inf, s)
        vals.append(m)
    return rank, vals


def _retrieve(s1, s2, cand_idx):
    key_idx = lax.broadcasted_iota(jnp.int32, s1.shape, 0).astype(F32)
    rank1, a = _extract_top(s1, key_idx, PEER_TOPK)
    rank2, b = _extract_top(s2, key_idx, PEER_TOPK)
    a_all = jnp.concatenate(a, axis=0)
    b_all = jnp.concatenate(b, axis=0)

    pieces = []
    for r, n in _CAND_SEGMENTS:
        pieces.append(a[r] + b_all[0:n, :])
    cand = jnp.concatenate(pieces, axis=0)
    crank, _ = _extract_top(cand, cand_idx, PEER_TOPK)
    chosen = crank < float(PEER_TOPK)

    cnt = []
    off = 0
    for r, n in _CAND_SEGMENTS:
        seg = jnp.where(chosen[off:off + n, :], 1.0, 0.0)
        cnt.append(jnp.sum(seg, axis=0, keepdims=True))
        off += n

    ea = jnp.exp(a_all - a[0])
    eb = jnp.exp(b_all - b[0])
    k_idx = lax.broadcasted_iota(jnp.int32, eb.shape, 0).astype(F32)
    z = jnp.zeros_like(a[0])
    for r in range(PEER_TOPK):
        prefix = jnp.sum(jnp.where(k_idx < cnt[r], eb, 0.0), axis=0, keepdims=True)
        z = z + ea[r:r + 1, :] * prefix
    inv_z = 1.0 / z

    cnt1 = jnp.zeros_like(s1)
    for r in range(PEER_TOPK):
        cnt1 = jnp.where(rank1 == float(r), cnt[r], cnt1)
    alpha = jnp.where(rank1 < float(PEER_TOPK), jnp.exp(s1 - a[0]) * inv_z, 0.0)
    beta = jnp.where(rank2 < float(PEER_TOPK), jnp.exp(s2 - b[0]), 0.0)
    return alpha, cnt1, beta, rank2


def _peer_kernel(x_ref, sh_ref, sc_ref, g_ref, wqt_ref, sk_ref, cidx_ref, u_ref, vt_ref,
                 lng_ref, lnb_ref, o_ref,
                 h2t_scr, ft_scr, s_scr, alpha_scr, cnt_scr, beta_scr, rho_scr,
                 at_scr, ct_scr, *, tile):
    e = pl.program_id(2)
    ne = pl.num_programs(2)
    n_tt = tile // LANES

    @pl.when(e == 0)
    def _prologue():
        x = x_ref[0]
        h2 = _ln_stats(x) * (1.0 + sc_ref[0]) + sh_ref[0]
        h2t = h2.T.astype(BF16)
        h2t_scr[...] = h2t
        ft_scr[...] = jnp.zeros(ft_scr.shape, F32)
        for h in range(PEER_HEADS):
            qt = _dot(wqt_ref[h * D_KEY:(h + 1) * D_KEY, :], h2t)
            for p in range(2):
                s = _dot(sk_ref[h, p], qt[p * D_HALF:(p + 1) * D_HALF, :].astype(BF16))
                for tt in range(n_tt):
                    s_scr[p, h, tt] = s[:, tt * LANES:(tt + 1) * LANES]

        cand_idx = cidx_ref[...]

        def topk_body(it, carry):
            h = it // n_tt
            tt = it % n_tt
            alpha, cnt1, beta, rank2 = _retrieve(s_scr[0, h, tt], s_scr[1, h, tt], cand_idx)
            alpha_scr[h, tt] = alpha
            cnt_scr[h, tt] = cnt1
            beta_scr[h, tt] = beta.astype(BF16)
            rho_scr[h, tt] = rank2.astype(BF16)
            return carry

        lax.fori_loop(0, PEER_HEADS * n_tt, topk_body, 0)

    at_scr[...] = _dot(u_ref[...], h2t_scr[...])

    n_grp = N_KEYS // BF16_ROWS

    def key_body(ib, carry):
        i = e * KEYS_PER_CHUNK + ib
        for tt in range(n_tt):
            lanes = slice(tt * LANES, (tt + 1) * LANES)
            al = []
            cc = []
            for h in range(PEER_HEADS):
                al.append(jnp.broadcast_to(alpha_scr[h, tt, pl.ds(i, 1), :],
                                           (BF16_ROWS, LANES)).astype(BF16))
                cc.append(jnp.broadcast_to(cnt_scr[h, tt, pl.ds(i, 1), :],
                                           (BF16_ROWS, LANES)).astype(BF16))
            for g in range(n_grp):
                grp = slice(g * BF16_ROWS, (g + 1) * BF16_ROWS)
                gate = None
                for h in range(PEER_HEADS):
                    hit = rho_scr[h, tt, grp, :] < cc[h]
                    term = al[h] * jnp.where(hit, beta_scr[h, tt, grp, :], jnp.zeros((), BF16))
                    gate = term if gate is None else gate + term
                rows = pl.ds(pl.multiple_of(ib * N_KEYS + g * BF16_ROWS, BF16_ROWS), BF16_ROWS)
                act = _gelu(at_scr[rows, lanes])
                ct_scr[rows, lanes] = act.astype(BF16) * gate
        return carry

    lax.fori_loop(0, KEYS_PER_CHUNK, key_body, 0)

    ft_scr[...] += _dot(vt_ref[...], ct_scr[...])

    @pl.when(e == ne - 1)
    def _epilogue():
        f = ft_scr[...].T
        y = ALPHA * x_ref[0] + g_ref[0] * f
        o_ref[0] = _ln_stats(y) * lng_ref[...] + lnb_ref[...]


def _peer(x, sh, sc, g, wqt, sk, cand_idx, u, vt, lng, lnb, *, tile):
    B, S, D = x.shape
    nt = S // tile
    ne = N_EXPERTS // EXPERT_CHUNK
    n_tt = tile // LANES
    R = sh.shape[1]
    if R == 1:
        mod_spec = pl.BlockSpec((1, 1, D), lambda b, t, e: (b, 0, 0))
    else:
        mod_spec = pl.BlockSpec((1, tile, D), lambda b, t, e: (b, t, 0))
    kern = functools.partial(_peer_kernel, tile=tile)
    return pl.pallas_call(
        kern,
        grid=(B, nt, ne),
        in_specs=[
            pl.BlockSpec((1, tile, D), lambda b, t, e: (b, t, 0)),
            mod_spec, mod_spec, mod_spec,
            pl.BlockSpec((PEER_HEADS * D_KEY, D), lambda b, t, e: (0, 0)),
            pl.BlockSpec((PEER_HEADS, 2, N_KEYS, D_HALF), lambda b, t, e: (0, 0, 0, 0)),
            pl.BlockSpec((N_CAND, LANES), lambda b, t, e: (0, 0)),
            pl.BlockSpec((EXPERT_CHUNK, D), lambda b, t, e: (e, 0)),
            pl.BlockSpec((D, EXPERT_CHUNK), lambda b, t, e: (0, e)),
            pl.BlockSpec((1, D), lambda b, t, e: (0, 0)),
            pl.BlockSpec((1, D), lambda b, t, e: (0, 0)),
        ],
        out_specs=pl.BlockSpec((1, tile, D), lambda b, t, e: (b, t, 0)),
        out_shape=jax.ShapeDtypeStruct((B, S, D), F32),
        scratch_shapes=[
            pltpu.VMEM((D, tile), BF16),
            pltpu.VMEM((D, tile), F32),
            pltpu.VMEM((2, PEER_HEADS, n_tt, N_KEYS, LANES), F32),
            pltpu.VMEM((PEER_HEADS, n_tt, N_KEYS, LANES), F32),
            pltpu.VMEM((PEER_HEADS, n_tt, N_KEYS, LANES), F32),
            pltpu.VMEM((PEER_HEADS, n_tt, N_KEYS, LANES), BF16),
            pltpu.VMEM((PEER_HEADS, n_tt, N_KEYS, LANES), BF16),
            pltpu.VMEM((EXPERT_CHUNK, tile), F32),
            pltpu.VMEM((EXPERT_CHUNK, tile), BF16),
        ],
        compiler_params=pltpu.CompilerParams(
            dimension_semantics=("arbitrary", "arbitrary", "arbitrary"),
            vmem_limit_bytes=VMEM_LIMIT_BYTES),
        name="peer",
    )(x, sh, sc, g, wqt, sk, cand_idx, u, vt, lng, lnb)


def _cand_index_table():
    idx = []
    for r, n in _CAND_SEGMENTS:
        idx.extend(r * PEER_TOPK + k for k in range(n))
    col = np.asarray(idx, np.float32)[:, None]
    return jnp.asarray(np.broadcast_to(col, (N_CAND, LANES)).copy())


def kernel(x_prompt, x_sample, state_conv, state_pool, c_prompt, c_sample, w_ada, b_ada, w_in, conv_w, pool_w, pool_scale, w_out, ln1_g, ln1_b, w_q, sub_keys, u_experts, v_experts, ln2_g, ln2_b):
    B, S, D = x_prompt.shape
    Bs = x_sample.shape[0]

    w_in_b = w_in.astype(BF16)
    w_out_b = w_out.astype(BF16)
    pool_w_b = pool_w.astype(BF16)
    wqt_b = jnp.swapaxes(w_q, 1, 2).astype(BF16)
    sk_b = sub_keys.astype(BF16)
    u_b = u_experts.astype(BF16)
    vt_b = jnp.swapaxes(v_experts, 1, 2).astype(BF16)
    cand_idx = _cand_index_table()

    c_all = jnp.concatenate([c_prompt, c_sample], axis=0)
    mod = _ada_modulation(c_all, w_ada, b_ada)

    xp = x_prompt
    xs = x_sample.reshape(Bs, D)
    conv_p, pool_p, conv_s, pool_s = [], [], [], []
    for l in range(DEPTH):
        mp = [mod[l, :B, k * D:(k + 1) * D].reshape(B, 1, D) for k in range(6)]
        ms = [mod[l, B:, k * D:(k + 1) * D] for k in range(6)]
        row = lambda a: a[l].reshape(1, -1)

        xp, cst, pst = _mix_prompt(xp, mp[0], mp[1], mp[2], w_in_b[l], conv_w[l], pool_w_b[l],
                                   row(pool_scale), w_out_b[l], row(ln1_g), row(ln1_b))
        conv_p.append(cst)
        pool_p.append(pst)
        xp = _peer(xp, mp[3], mp[4], mp[5], wqt_b[l], sk_b[l], cand_idx, u_b[l], vt_b[l],
                   row(ln2_g), row(ln2_b), tile=PEER_TILE)

        xs, cst, pst = _mix_sample(
            xs, ms[0], ms[1], ms[2],
            state_conv[l].reshape(Bs, (CONV_W - 1) * D_CONV),
            state_pool[l].reshape(Bs, (POOL_MAX - 1) * D_POOL),
            w_in_b[l], conv_w[l], pool_w_b[l], row(pool_scale), w_out_b[l],
            row(ln1_g), row(ln1_b))
        conv_s.append(cst.reshape(Bs, CONV_W - 1, D_CONV))
        pool_s.append(pst.reshape(Bs, POOL_MAX - 1, D_POOL))
        xs = _peer(xs.reshape(1, Bs, D), ms[3].reshape(1, Bs, D), ms[4].reshape(1, Bs, D),
                   ms[5].reshape(1, Bs, D), wqt_b[l], sk_b[l], cand_idx, u_b[l], vt_b[l],
                   row(ln2_g), row(ln2_b), tile=Bs).reshape(Bs, D)

    return (xp, xs.reshape(Bs, 1, D), jnp.stack(conv_p), jnp.stack(pool_p),
            jnp.stack(conv_s), jnp.stack(pool_s))
```

```python
import functools
import math

import jax
import jax.numpy as jnp
import numpy as np
from jax import lax
from jax.experimental import pallas as pl
from jax.experimental.pallas import tpu as pltpu

F32 = jnp.float32
BF16 = jnp.bfloat16

D_MODEL = 1024
DEPTH = 2
D_CONV = D_MODEL // 2
D_POOL = D_MODEL - D_CONV
CONV_W = 3
POOL_WINDOWS = (2, 4, 8, 16)
POOL_GROUP = D_POOL // len(POOL_WINDOWS)
POOL_MAX = max(POOL_WINDOWS)
D_IN = 3 * D_CONV + D_POOL
PEER_HEADS = 8
N_KEYS = 128
N_EXPERTS = N_KEYS * N_KEYS
PEER_TOPK = 16
D_KEY = 256
D_HALF = D_KEY // 2
PAST_LEN = 16384
ALPHA = (2 * DEPTH) ** 0.25
LN_EPS = 1e-5

LANES = 128
SUBLANES = 8
BF16_ROWS = 16
MXU_COLS = 256
VMEM_LIMIT_BYTES = 56 * 1024 * 1024

MIX_TILE = 512
PEER_TILE = 512
EXPERT_CHUNK = 1024
KEYS_PER_CHUNK = EXPERT_CHUNK // N_KEYS
ADA_COLS = 1536

_CAND_SEGMENTS = ((0, 16), (1, 8), (2, 8), (3, 8), (4, 4), (5, 4), (6, 4), (7, 4)) + tuple(
    (r, 1) for r in range(8, 16))
N_CAND = sum(n for _, n in _CAND_SEGMENTS)
assert N_CAND == 64


def _ln_stats(x):
    mu = jnp.mean(x, axis=-1, keepdims=True)
    xc = x - mu
    var = jnp.mean(xc * xc, axis=-1, keepdims=True)
    return xc * lax.rsqrt(var + LN_EPS)


def _gelu(x):
    return 0.5 * x * (1.0 + lax.erf(x * (1.0 / math.sqrt(2.0))))


def _dot(a, b):
    return jnp.dot(a, b, preferred_element_type=F32)


def _ada_kernel(c_ref, w_ref, b_ref, o_ref):
    c = c_ref[...]
    act = (c * jax.nn.sigmoid(c)).astype(BF16)
    o_ref[0] = _dot(act, w_ref[0].astype(BF16)) + b_ref[0]


def _ada_modulation(c_all, w_ada, b_ada):
    n = c_all.shape[0]
    n_cols = 6 * D_MODEL
    return pl.pallas_call(
        _ada_kernel,
        grid=(DEPTH, n_cols // ADA_COLS),
        in_specs=[
            pl.BlockSpec((n, D_MODEL), lambda l, j: (0, 0)),
            pl.BlockSpec((1, D_MODEL, ADA_COLS), lambda l, j: (l, 0, j)),
            pl.BlockSpec((1, 1, ADA_COLS), lambda l, j: (l, 0, j)),
        ],
        out_specs=pl.BlockSpec((1, n, ADA_COLS), lambda l, j: (l, 0, j)),
        out_shape=jax.ShapeDtypeStruct((DEPTH, n, n_cols), F32),
        compiler_params=pltpu.CompilerParams(
            dimension_semantics=("arbitrary", "arbitrary"),
            vmem_limit_bytes=VMEM_LIMIT_BYTES),
        name="ada_modulation",
    )(c_all, w_ada, b_ada.reshape(DEPTH, 1, n_cols))


def _mix_tail(x, g1, conv_out, pooled, poolw_ref, pscale, wout_ref, lng, lnb):
    mixed = [
        _dot(pooled[g].astype(BF16), poolw_ref[g]) for g in range(len(POOL_WINDOWS))
    ]
    pool_out = jnp.concatenate(mixed, axis=-1) * pscale
    cat = jnp.concatenate([conv_out, pool_out], axis=-1).astype(BF16)
    mix = _dot(cat, wout_ref[...])
    return _ln_stats(ALPHA * x + g1 * mix) * lng + lnb


def _mix_prompt_kernel(x_ref, sh_ref, sc_ref, g_ref, win_ref, cw_ref, poolw_ref,
                       pscale_ref, wout_ref, lng_ref, lnb_ref,
                       o_ref, cst_ref, pst_ref, ue_scr, pe_scr):
    j = pl.program_id(1)
    nt = pl.num_programs(1)
    T = MIX_TILE
    HU = SUBLANES
    HP = POOL_MAX

    @pl.when(j == 0)
    def _():
        ue_scr[0:HU, :] = jnp.zeros((HU, D_CONV), F32)
        pe_scr[0:HP, :] = jnp.zeros((HP, D_POOL), F32)

    x = x_ref[0]
    h = _ln_stats(x) * (1.0 + sc_ref[0]) + sh_ref[0]
    z = _dot(h.astype(BF16), win_ref[...])
    bg = z[:, 0:D_CONV]
    cg = z[:, D_CONV:2 * D_CONV]
    hc = z[:, 2 * D_CONV:3 * D_CONV]
    hp = z[:, 3 * D_CONV:]

    u = cg * hc
    ue_scr[HU:HU + T, :] = u
    pe_scr[HP:HP + T, :] = hp

    cw = cw_ref[...]
    y = (ue_scr[HU - 2:HU - 2 + T, :] * cw[0:1, :]
         + ue_scr[HU - 1:HU - 1 + T, :] * cw[1:2, :]
         + u * cw[2:3, :])
    conv_out = bg * y

    pos = (lax.broadcasted_iota(jnp.int32, (T, POOL_GROUP), 0) + j * T).astype(F32)
    pooled = []
    for g, w in enumerate(POOL_WINDOWS):
        lanes = slice(g * POOL_GROUP, (g + 1) * POOL_GROUP)
        win = hp[:, lanes]
        for k in range(1, w):
            win = win + pe_scr[HP - k:HP - k + T, lanes]
        cnt = jnp.minimum(pos + 1.0, float(w))
        pooled.append(win / cnt - hp[:, lanes])

    o_ref[0] = _mix_tail(x, g_ref[0], conv_out, pooled, poolw_ref, pscale_ref[...],
                         wout_ref, lng_ref[...], lnb_ref[...])

    @pl.when(j == nt - 1)
    def _():
        cst_ref[0] = ue_scr[HU + T - (CONV_W - 1):HU + T, :]
        pst_ref[0] = pe_scr[HP + T - (POOL_MAX - 1):HP + T, :]

    @pl.when(j < nt - 1)
    def _():
        ue_scr[0:HU, :] = ue_scr[T:T + HU, :]
        pe_scr[0:HP, :] = pe_scr[T:T + HP, :]


def _mix_prompt(x, sh, sc, g, win, cw, poolw, pscale, wout, lng, lnb):
    B, S, D = x.shape
    nt = S // MIX_TILE
    row = lambda b, j: (b, 0, 0)
    full2 = lambda b, j: (0, 0)
    full3 = lambda b, j: (0, 0, 0)
    return pl.pallas_call(
        _mix_prompt_kernel,
        grid=(B, nt),
        in_specs=[
            pl.BlockSpec((1, MIX_TILE, D), lambda b, j: (b, j, 0)),
            pl.BlockSpec((1, 1, D), row),
            pl.BlockSpec((1, 1, D), row),
            pl.BlockSpec((1, 1, D), row),
            pl.BlockSpec((D, D_IN), full2),
            pl.BlockSpec((CONV_W, D_CONV), full2),
            pl.BlockSpec((len(POOL_WINDOWS), POOL_GROUP, POOL_GROUP), full3),
            pl.BlockSpec((1, D_POOL), full2),
            pl.BlockSpec((D, D), full2),
            pl.BlockSpec((1, D), full2),
            pl.BlockSpec((1, D), full2),
        ],
        out_specs=[
            pl.BlockSpec((1, MIX_TILE, D), lambda b, j: (b, j, 0)),
            pl.BlockSpec((1, CONV_W - 1, D_CONV), row),
            pl.BlockSpec((1, POOL_MAX - 1, D_POOL), row),
        ],
        out_shape=[
            jax.ShapeDtypeStruct((B, S, D), F32),
            jax.ShapeDtypeStruct((B, CONV_W - 1, D_CONV), F32),
            jax.ShapeDtypeStruct((B, POOL_MAX - 1, D_POOL), F32),
        ],
        scratch_shapes=[
            pltpu.VMEM((MIX_TILE + SUBLANES, D_CONV), F32),
            pltpu.VMEM((MIX_TILE + POOL_MAX, D_POOL), F32),
        ],
        compiler_params=pltpu.CompilerParams(
            dimension_semantics=("arbitrary", "arbitrary"),
            vmem_limit_bytes=VMEM_LIMIT_BYTES),
        name="mix_prompt",
    )(x, sh, sc, g, win, cw, poolw, pscale, wout, lng, lnb)


def _mix_sample_kernel(x_ref, sh_ref, sc_ref, g_ref, cst_in_ref, pst_in_ref, win_ref,
                       cw_ref, poolw_ref, pscale_ref, wout_ref, lng_ref, lnb_ref,
                       o_ref, cst_ref, pst_ref):
    x = x_ref[...]
    h = _ln_stats(x) * (1.0 + sc_ref[...]) + sh_ref[...]
    z = _dot(h.astype(BF16), win_ref[...])
    bg = z[:, 0:D_CONV]
    cg = z[:, D_CONV:2 * D_CONV]
    hc = z[:, 2 * D_CONV:3 * D_CONV]
    hp = z[:, 3 * D_CONV:]

    u = cg * hc
    cw = cw_ref[...]
    prev2 = cst_in_ref[:, 0:D_CONV]
    prev1 = cst_in_ref[:, D_CONV:2 * D_CONV]
    conv_out = bg * (prev2 * cw[0:1, :] + prev1 * cw[1:2, :] + u * cw[2:3, :])
    cst_ref[:, 0:D_CONV] = prev1
    cst_ref[:, D_CONV:2 * D_CONV] = u

    n_hist = POOL_MAX - 1
    pooled = []
    for g, w in enumerate(POOL_WINDOWS):
        win = hp[:, g * POOL_GROUP:(g + 1) * POOL_GROUP]
        for k in range(1, w):
            lo = (n_hist - k) * D_POOL + g * POOL_GROUP
            win = win + pst_in_ref[:, lo:lo + POOL_GROUP]
        cnt = min(float(PAST_LEN) + 1.0, float(w))
        pooled.append(win / cnt - hp[:, g * POOL_GROUP:(g + 1) * POOL_GROUP])
    pst_ref[:, 0:(n_hist - 1) * D_POOL] = pst_in_ref[:, D_POOL:n_hist * D_POOL]
    pst_ref[:, (n_hist - 1) * D_POOL:n_hist * D_POOL] = hp

    o_ref[...] = _mix_tail(x, g_ref[...], conv_out, pooled, poolw_ref, pscale_ref[...],
                           wout_ref, lng_ref[...], lnb_ref[...])


def _mix_sample(x, sh, sc, g, cst, pst, win, cw, poolw, pscale, wout, lng, lnb):
    B, D = x.shape
    return pl.pallas_call(
        _mix_sample_kernel,
        out_shape=[
            jax.ShapeDtypeStruct((B, D), F32),
            jax.ShapeDtypeStruct(cst.shape, F32),
            jax.ShapeDtypeStruct(pst.shape, F32),
        ],
        compiler_params=pltpu.CompilerParams(vmem_limit_bytes=VMEM_LIMIT_BYTES),
        name="mix_sample",
    )(x, sh, sc, g, cst, pst, win, cw, poolw, pscale, wout, lng, lnb)


def _extract_top(s, idx, n_out):
    big = float(4 * N_EXPERTS)
    rank = jnp.full(s.shape, float(n_out), F32)
    vals = []
    for step in range(n_out):
        m = jnp.max(s, axis=0, keepdims=True)
        first = jnp.min(jnp.where(s == m, idx, big), axis=0, keepdims=True)
        sel = idx == first
        rank = jnp.where(sel, float(step), rank)
        s = jnp.where(sel, -jnp.inf, s)
        vals.append(m)
    return rank, vals


def _retrieve(s1, s2, cand_idx):
    key_idx = lax.broadcasted_iota(jnp.int32, s1.shape, 0).astype(F32)
    rank1, a = _extract_top(s1, key_idx, PEER_TOPK)
    rank2, b = _extract_top(s2, key_idx, PEER_TOPK)
    a_all = jnp.concatenate(a, axis=0)
    b_all = jnp.concatenate(b, axis=0)

    pieces = []
    for r, n in _CAND_SEGMENTS:
        pieces.append(a[r] + b_all[0:n, :])
    cand = jnp.concatenate(pieces, axis=0)
    crank, _ = _extract_top(cand, cand_idx, PEER_TOPK)
    chosen = crank < float(PEER_TOPK)

    cnt = []
    off = 0
    for r, n in _CAND_SEGMENTS:
        seg = jnp.where(chosen[off:off + n, :], 1.0, 0.0)
        cnt.append(jnp.sum(seg, axis=0, keepdims=True))
        off += n

    ea = jnp.exp(a_all - a[0])
    eb = jnp.exp(b_all - b[0])
    k_idx = lax.broadcasted_iota(jnp.int32, eb.shape, 0).astype(F32)
    z = jnp.zeros_like(a[0])
    for r in range(PEER_TOPK):
        prefix = jnp.sum(jnp.where(k_idx < cnt[r], eb, 0.0), axis=0, keepdims=True)
        z = z + ea[r:r + 1, :] * prefix
    inv_z = 1.0 / z

    cnt1 = jnp.zeros_like(s1)
    for r in range(PEER_TOPK):
        cnt1 = jnp.where(rank1 == float(r), cnt[r], cnt1)
    alpha = jnp.where(rank1 < float(PEER_TOPK), jnp.exp(s1 - a[0]) * inv_z, 0.0)
    beta = jnp.where(rank2 < float(PEER_TOPK), jnp.exp(s2 - b[0]), 0.0)
    return alpha, cnt1, beta, rank2


def _gate_key_block(i, ib, n_tt, per_nb, at_cur, ct_cur, alpha_scr, cnt_scr, beta_scr, rho_scr):
    n_grp = N_KEYS // BF16_ROWS
    for tt in range(n_tt):
        nb, part = divmod(tt, per_nb)
        lanes = slice(part * LANES, (part + 1) * LANES)
        gate = [None] * n_grp
        for h in range(PEER_HEADS):
            al = jnp.broadcast_to(alpha_scr[h, tt, pl.ds(i, 1), :], (BF16_ROWS, LANES)).astype(BF16)
            cc = jnp.broadcast_to(cnt_scr[h, tt, pl.ds(i, 1), :], (BF16_ROWS, LANES)).astype(BF16)
            for g in range(n_grp):
                grp = slice(g * BF16_ROWS, (g + 1) * BF16_ROWS)
                hit = rho_scr[h, tt, grp, :] < cc
                term = al * jnp.where(hit, beta_scr[h, tt, grp, :], jnp.zeros((), BF16))
                gate[g] = term if gate[g] is None else gate[g] + term
        for g in range(n_grp):
            rows = slice(ib * N_KEYS + g * BF16_ROWS, ib * N_KEYS + (g + 1) * BF16_ROWS)
            t = at_cur[nb, rows, lanes]
            act = t + t * lax.erf(t)
            ct_cur[nb, rows, lanes] = act.astype(BF16) * gate[g]


def _peer_kernel(x_ref, sh_ref, sc_ref, g_ref, wqt_ref, sk_ref, cidx_ref, u_ref, vt_ref,
                 lng_ref, lnb_ref, o_ref,
                 h2s_scr, h2g_scr, ft_scr, s_scr, alpha_scr, cnt_scr, beta_scr, rho_scr,
                 at0_scr, at1_scr, ct0_scr, ct1_scr, *, tile, width):
    e = pl.program_id(2)
    n_chunks = pl.num_programs(2) - 2
    n_tt = tile // LANES
    n_nb = tile // width
    per_nb = width // LANES
    n_rb = KEYS_PER_CHUNK // n_nb
    u_rows = EXPERT_CHUNK // n_rb
    f_rows = D_MODEL // n_rb

    @pl.when(e == 0)
    def _first():
        x = x_ref[0]
        h2 = _ln_stats(x) * (1.0 + sc_ref[0]) + sh_ref[0]
        h2t = h2.T
        for nb in range(n_nb):
            blk = h2t[:, nb * width:(nb + 1) * width]
            h2s_scr[nb] = blk.astype(BF16)
            h2g_scr[nb] = (blk * math.sqrt(0.5)).astype(BF16)
        ft_scr[...] = jnp.zeros(ft_scr.shape, F32)
        ct1_scr[...] = jnp.zeros(ct1_scr.shape, BF16)
        for h in range(PEER_HEADS):
            for nb in range(n_nb):
                qt = _dot(wqt_ref[h * D_KEY:(h + 1) * D_KEY, :], h2s_scr[nb])
                for p in range(2):
                    s = _dot(sk_ref[h, p], qt[p * D_HALF:(p + 1) * D_HALF, :].astype(BF16))
                    for part in range(per_nb):
                        s_scr[p, h, nb * per_nb + part] = s[:, part * LANES:(part + 1) * LANES]

        cand_idx = cidx_ref[...]

        def topk_body(it, carry):
            h = it // n_tt
            tt = it % n_tt
            alpha, cnt1, beta, rank2 = _retrieve(s_scr[0, h, tt], s_scr[1, h, tt], cand_idx)
            alpha_scr[h, tt] = alpha * math.sqrt(0.5)
            cnt_scr[h, tt] = cnt1
            beta_scr[h, tt] = beta.astype(BF16)
            rho_scr[h, tt] = rank2.astype(BF16)
            return carry

        lax.fori_loop(0, PEER_HEADS * n_tt, topk_body, 0)

        for nb in range(n_nb):
            at0_scr[nb] = _dot(u_ref[...], h2g_scr[nb])

    def steady(at_new, at_cur, ct_cur, ct_old):
        for ib in range(KEYS_PER_CHUNK):
            nb, rb = divmod(ib, n_rb)
            urows = slice(rb * u_rows, (rb + 1) * u_rows)
            frows = slice(rb * f_rows, (rb + 1) * f_rows)
            at_new[nb, urows, :] = _dot(u_ref[urows, :], h2g_scr[nb])
            _gate_key_block((e - 1) * KEYS_PER_CHUNK + ib, ib, n_tt, per_nb, at_cur, ct_cur,
                            alpha_scr, cnt_scr, beta_scr, rho_scr)
            ft_scr[nb, frows, :] += _dot(vt_ref[frows, :], ct_old[nb])

    in_steady = jnp.logical_and(e >= 1, e <= n_chunks)

    @pl.when(jnp.logical_and(in_steady, e % 2 == 1))
    def _odd():
        steady(at1_scr, at0_scr, ct0_scr, ct1_scr)

    @pl.when(jnp.logical_and(in_steady, e % 2 == 0))
    def _even():
        steady(at0_scr, at1_scr, ct1_scr, ct0_scr)

    @pl.when(e == n_chunks + 1)
    def _last():
        for nb in range(n_nb):
            f = (ft_scr[nb] + _dot(vt_ref[...], ct1_scr[nb])).T
            rows = slice(nb * width, (nb + 1) * width)
            g = g_ref[0] if g_ref.shape[1] == 1 else g_ref[0, rows, :]
            y = ALPHA * x_ref[0, rows, :] + g * f
            o_ref[0, rows, :] = _ln_stats(y) * lng_ref[...] + lnb_ref[...]


def _peer(x, sh, sc, g, wqt, sk, cand_idx, u, vt, lng, lnb, *, tile):
    B, S, D = x.shape
    nt = S // tile
    ne = N_EXPERTS // EXPERT_CHUNK
    assert ne % 2 == 0
    n_tt = tile // LANES
    width = min(tile, MXU_COLS)
    n_nb = tile // width
    R = sh.shape[1]
    if R == 1:
        mod_spec = pl.BlockSpec((1, 1, D), lambda b, t, e: (b, 0, 0))
    else:
        mod_spec = pl.BlockSpec((1, tile, D), lambda b, t, e: (b, t, 0))
    kern = functools.partial(_peer_kernel, tile=tile, width=width)
    return pl.pallas_call(
        kern,
        grid=(B, nt, ne + 2),
        in_specs=[
            pl.BlockSpec((1, tile, D), lambda b, t, e: (b, t, 0)),
            mod_spec, mod_spec, mod_spec,
            pl.BlockSpec((PEER_HEADS * D_KEY, D), lambda b, t, e: (0, 0)),
            pl.BlockSpec((PEER_HEADS, 2, N_KEYS, D_HALF), lambda b, t, e: (0, 0, 0, 0)),
            pl.BlockSpec((N_CAND, LANES), lambda b, t, e: (0, 0)),
            pl.BlockSpec((EXPERT_CHUNK, D), lambda b, t, e: (jnp.minimum(e, ne - 1), 0)),
            pl.BlockSpec((D, EXPERT_CHUNK), lambda b, t, e: (0, jnp.clip(e - 2, 0, ne - 1))),
            pl.BlockSpec((1, D), lambda b, t, e: (0, 0)),
            pl.BlockSpec((1, D), lambda b, t, e: (0, 0)),
        ],
        out_specs=pl.BlockSpec((1, tile, D), lambda b, t, e: (b, t, 0)),
        out_shape=jax.ShapeDtypeStruct((B, S, D), F32),
        scratch_shapes=[
            pltpu.VMEM((n_nb, D, width), BF16),
            pltpu.VMEM((n_nb, D, width), BF16),
            pltpu.VMEM((n_nb, D, width), F32),
            pltpu.VMEM((2, PEER_HEADS, n_tt, N_KEYS, LANES), F32),
            pltpu.VMEM((PEER_HEADS, n_tt, N_KEYS, LANES), F32),
            pltpu.VMEM((PEER_HEADS, n_tt, N_KEYS, LANES), F32),
            pltpu.VMEM((PEER_HEADS, n_tt, N_KEYS, LANES), BF16),
            pltpu.VMEM((PEER_HEADS, n_tt, N_KEYS, LANES), BF16),
            pltpu.VMEM((n_nb, EXPERT_CHUNK, width), F32),
            pltpu.VMEM((n_nb, EXPERT_CHUNK, width), F32),
            pltpu.VMEM((n_nb, EXPERT_CHUNK, width), BF16),
            pltpu.VMEM((n_nb, EXPERT_CHUNK, width), BF16),
        ],
        compiler_params=pltpu.CompilerParams(
            dimension_semantics=("arbitrary", "arbitrary", "arbitrary"),
            vmem_limit_bytes=VMEM_LIMIT_BYTES),
        name="peer",
    )(x, sh, sc, g, wqt, sk, cand_idx, u, vt, lng, lnb)


def _cand_index_table():
    idx = []
    for r, n in _CAND_SEGMENTS:
        idx.extend(r * PEER_TOPK + k for k in range(n))
    col = np.asarray(idx, np.float32)[:, None]
    return jnp.asarray(np.broadcast_to(col, (N_CAND, LANES)).copy())


def kernel(x_prompt, x_sample, state_conv, state_pool, c_prompt, c_sample, w_ada, b_ada, w_in, conv_w, pool_w, pool_scale, w_out, ln1_g, ln1_b, w_q, sub_keys, u_experts, v_experts, ln2_g, ln2_b):
    B, S, D = x_prompt.shape
    Bs = x_sample.shape[0]

    w_in_b = w_in.astype(BF16)
    w_out_b = w_out.astype(BF16)
    pool_w_b = pool_w.astype(BF16)
    wqt_b = jnp.swapaxes(w_q, 1, 2).astype(BF16)
    sk_b = sub_keys.astype(BF16)
    u_b = u_experts.astype(BF16)
    vt_b = jnp.swapaxes(v_experts, 1, 2).astype(BF16)
    cand_idx = _cand_index_table()

    c_all = jnp.concatenate([c_prompt, c_sample], axis=0)
    mod = _ada_modulation(c_all, w_ada, b_ada)

    xp = x_prompt
    xs = x_sample.reshape(Bs, D)
    conv_p, pool_p, conv_s, pool_s = [], [], [], []
    for l in range(DEPTH):
        mp = [mod[l, :B, k * D:(k + 1) * D].reshape(B, 1, D) for k in range(6)]
        ms = [mod[l, B:, k * D:(k + 1) * D] for k in range(6)]
        row = lambda a: a[l].reshape(1, -1)

        xp, cst, pst = _mix_prompt(xp, mp[0], mp[1], mp[2], w_in_b[l], conv_w[l], pool_w_b[l],
                                   row(pool_scale), w_out_b[l], row(ln1_g), row(ln1_b))
        conv_p.append(cst)
        pool_p.append(pst)
        xp = _peer(xp, mp[3], mp[4], mp[5], wqt_b[l], sk_b[l], cand_idx, u_b[l], vt_b[l],
                   row(ln2_g), row(ln2_b), tile=PEER_TILE)

        xs, cst, pst = _mix_sample(
            xs, ms[0], ms[1], ms[2],
            state_conv[l].reshape(Bs, (CONV_W - 1) * D_CONV),
            state_pool[l].reshape(Bs, (POOL_MAX - 1) * D_POOL),
            w_in_b[l], conv_w[l], pool_w_b[l], row(pool_scale), w_out_b[l],
            row(ln1_g), row(ln1_b))
        conv_s.append(cst.reshape(Bs, CONV_W - 1, D_CONV))
        pool_s.append(pst.reshape(Bs, POOL_MAX - 1, D_POOL))
        xs = _peer(xs.reshape(1, Bs, D), ms[3].reshape(1, Bs, D), ms[4].reshape(1, Bs, D),
                   ms[5].reshape(1, Bs, D), wqt_b[l], sk_b[l], cand_idx, u_b[l], vt_b[l],
                   row(ln2_g), row(ln2_b), tile=Bs).reshape(Bs, D)

    return (xp, xs.reshape(Bs, 1, D), jnp.stack(conv_p), jnp.stack(pool_p),
            jnp.stack(conv_s), jnp.stack(pool_s))
```

```python
import functools
import math

import jax
import jax.numpy as jnp
import numpy as np
from jax import lax
from jax.experimental import pallas as pl
from jax.experimental.pallas import tpu as pltpu

F32 = jnp.float32
BF16 = jnp.bfloat16

D_MODEL = 1024
DEPTH = 2
D_CONV = D_MODEL // 2
D_POOL = D_MODEL - D_CONV
CONV_W = 3
POOL_WINDOWS = (2, 4, 8, 16)
POOL_GROUP = D_POOL // len(POOL_WINDOWS)
POOL_MAX = max(POOL_WINDOWS)
D_IN = 3 * D_CONV + D_POOL
PEER_HEADS = 8
N_KEYS = 128
N_EXPERTS = N_KEYS * N_KEYS
PEER_TOPK = 16
D_KEY = 256
D_HALF = D_KEY // 2
PAST_LEN = 16384
ALPHA = (2 * DEPTH) ** 0.25
LN_EPS = 1e-5

LANES = 128
SUBLANES = 8
BF16_ROWS = 16
MXU_COLS = 256
VMEM_LIMIT_BYTES = 56 * 1024 * 1024

MIX_TILE = 512
PEER_TILE = 512
EXPERT_CHUNK = 1024
KEYS_PER_CHUNK = EXPERT_CHUNK // N_KEYS
ADA_COLS = 1536

_CAND_SEGMENTS = ((0, 16), (1, 8), (2, 8), (3, 8), (4, 4), (5, 4), (6, 4), (7, 4)) + tuple(
    (r, 1) for r in range(8, 16))
N_CAND = sum(n for _, n in _CAND_SEGMENTS)
assert N_CAND == 64


def _ln_stats(x):
    mu = jnp.mean(x, axis=-1, keepdims=True)
    xc = x - mu
    var = jnp.mean(xc * xc, axis=-1, keepdims=True)
    return xc * lax.rsqrt(var + LN_EPS)


def _gelu(x):
    return 0.5 * x * (1.0 + lax.erf(x * (1.0 / math.sqrt(2.0))))


def _dot(a, b):
    return jnp.dot(a, b, preferred_element_type=F32)


def _ada_kernel(c_ref, w_ref, b_ref, o_ref):
    c = c_ref[...]
    act = (c * jax.nn.sigmoid(c)).astype(BF16)
    o_ref[0] = _dot(act, w_ref[0].astype(BF16)) + b_ref[0]


def _ada_modulation(c_all, w_ada, b_ada):
    n = c_all.shape[0]
    n_cols = 6 * D_MODEL
    return pl.pallas_call(
        _ada_kernel,
        grid=(DEPTH, n_cols // ADA_COLS),
        in_specs=[
            pl.BlockSpec((n, D_MODEL), lambda l, j: (0, 0)),
            pl.BlockSpec((1, D_MODEL, ADA_COLS), lambda l, j: (l, 0, j)),
            pl.BlockSpec((1, 1, ADA_COLS), lambda l, j: (l, 0, j)),
        ],
        out_specs=pl.BlockSpec((1, n, ADA_COLS), lambda l, j: (l, 0, j)),
        out_shape=jax.ShapeDtypeStruct((DEPTH, n, n_cols), F32),
        compiler_params=pltpu.CompilerParams(
            dimension_semantics=("arbitrary", "arbitrary"),
            vmem_limit_bytes=VMEM_LIMIT_BYTES),
        name="ada_modulation",
    )(c_all, w_ada, b_ada.reshape(DEPTH, 1, n_cols))


def _mix_tail(x, g1, conv_out, pooled, poolw_ref, pscale, wout_ref, lng, lnb):
    mixed = [
        _dot(pooled[g].astype(BF16), poolw_ref[g]) for g in range(len(POOL_WINDOWS))
    ]
    pool_out = jnp.concatenate(mixed, axis=-1) * pscale
    cat = jnp.concatenate([conv_out, pool_out], axis=-1).astype(BF16)
    mix = _dot(cat, wout_ref[...])
    return _ln_stats(ALPHA * x + g1 * mix) * lng + lnb


def _mix_prompt_kernel(x_ref, sh_ref, sc_ref, g_ref, win_ref, cw_ref, poolw_ref,
                       pscale_ref, wout_ref, lng_ref, lnb_ref,
                       o_ref, cst_ref, pst_ref, ue_scr, pe_scr):
    j = pl.program_id(1)
    nt = pl.num_programs(1)
    T = MIX_TILE
    HU = SUBLANES
    HP = POOL_MAX

    @pl.when(j == 0)
    def _():
        ue_scr[0:HU, :] = jnp.zeros((HU, D_CONV), F32)
        pe_scr[0:HP, :] = jnp.zeros((HP, D_POOL), F32)

    x = x_ref[0]
    h = _ln_stats(x) * (1.0 + sc_ref[0]) + sh_ref[0]
    z = _dot(h.astype(BF16), win_ref[...])
    bg = z[:, 0:D_CONV]
    cg = z[:, D_CONV:2 * D_CONV]
    hc = z[:, 2 * D_CONV:3 * D_CONV]
    hp = z[:, 3 * D_CONV:]

    u = cg * hc
    ue_scr[HU:HU + T, :] = u
    pe_scr[HP:HP + T, :] = hp

    cw = cw_ref[...]
    y = (ue_scr[HU - 2:HU - 2 + T, :] * cw[0:1, :]
         + ue_scr[HU - 1:HU - 1 + T, :] * cw[1:2, :]
         + u * cw[2:3, :])
    conv_out = bg * y

    pos = (lax.broadcasted_iota(jnp.int32, (T, POOL_GROUP), 0) + j * T).astype(F32)
    pooled = []
    for g, w in enumerate(POOL_WINDOWS):
        lanes = slice(g * POOL_GROUP, (g + 1) * POOL_GROUP)
        win = hp[:, lanes]
        for k in range(1, w):
            win = win + pe_scr[HP - k:HP - k + T, lanes]
        cnt = jnp.minimum(pos + 1.0, float(w))
        pooled.append(win / cnt - hp[:, lanes])

    o_ref[0] = _mix_tail(x, g_ref[0], conv_out, pooled, poolw_ref, pscale_ref[...],
                         wout_ref, lng_ref[...], lnb_ref[...])

    @pl.when(j == nt - 1)
    def _():
        cst_ref[0] = ue_scr[HU + T - (CONV_W - 1):HU + T, :]
        pst_ref[0] = pe_scr[HP + T - (POOL_MAX - 1):HP + T, :]

    @pl.when(j < nt - 1)
    def _():
        ue_scr[0:HU, :] = ue_scr[T:T + HU, :]
        pe_scr[0:HP, :] = pe_scr[T:T + HP, :]


def _mix_prompt(x, sh, sc, g, win, cw, poolw, pscale, wout, lng, lnb):
    B, S, D = x.shape
    nt = S // MIX_TILE
    row = lambda b, j: (b, 0, 0)
    full2 = lambda b, j: (0, 0)
    full3 = lambda b, j: (0, 0, 0)
    return pl.pallas_call(
        _mix_prompt_kernel,
        grid=(B, nt),
        in_specs=[
            pl.BlockSpec((1, MIX_TILE, D), lambda b, j: (b, j, 0)),
            pl.BlockSpec((1, 1, D), row),
            pl.BlockSpec((1, 1, D), row),
            pl.BlockSpec((1, 1, D), row),
            pl.BlockSpec((D, D_IN), full2),
            pl.BlockSpec((CONV_W, D_CONV), full2),
            pl.BlockSpec((len(POOL_WINDOWS), POOL_GROUP, POOL_GROUP), full3),
            pl.BlockSpec((1, D_POOL), full2),
            pl.BlockSpec((D, D), full2),
            pl.BlockSpec((1, D), full2),
            pl.BlockSpec((1, D), full2),
        ],
        out_specs=[
            pl.BlockSpec((1, MIX_TILE, D), lambda b, j: (b, j, 0)),
            pl.BlockSpec((1, CONV_W - 1, D_CONV), row),
            pl.BlockSpec((1, POOL_MAX - 1, D_POOL), row),
        ],
        out_shape=[
            jax.ShapeDtypeStruct((B, S, D), F32),
            jax.ShapeDtypeStruct((B, CONV_W - 1, D_CONV), F32),
            jax.ShapeDtypeStruct((B, POOL_MAX - 1, D_POOL), F32),
        ],
        scratch_shapes=[
            pltpu.VMEM((MIX_TILE + SUBLANES, D_CONV), F32),
            pltpu.VMEM((MIX_TILE + POOL_MAX, D_POOL), F32),
        ],
        compiler_params=pltpu.CompilerParams(
            dimension_semantics=("arbitrary", "arbitrary"),
            vmem_limit_bytes=VMEM_LIMIT_BYTES),
        name="mix_prompt",
    )(x, sh, sc, g, win, cw, poolw, pscale, wout, lng, lnb)


def _mix_sample_kernel(x_ref, sh_ref, sc_ref, g_ref, cst_in_ref, pst_in_ref, win_ref,
                       cw_ref, poolw_ref, pscale_ref, wout_ref, lng_ref, lnb_ref,
                       o_ref, cst_ref, pst_ref):
    x = x_ref[...]
    h = _ln_stats(x) * (1.0 + sc_ref[...]) + sh_ref[...]
    z = _dot(h.astype(BF16), win_ref[...])
    bg = z[:, 0:D_CONV]
    cg = z[:, D_CONV:2 * D_CONV]
    hc = z[:, 2 * D_CONV:3 * D_CONV]
    hp = z[:, 3 * D_CONV:]

    u = cg * hc
    cw = cw_ref[...]
    prev2 = cst_in_ref[:, 0:D_CONV]
    prev1 = cst_in_ref[:, D_CONV:2 * D_CONV]
    conv_out = bg * (prev2 * cw[0:1, :] + prev1 * cw[1:2, :] + u * cw[2:3, :])
    cst_ref[:, 0:D_CONV] = prev1
    cst_ref[:, D_CONV:2 * D_CONV] = u

    n_hist = POOL_MAX - 1
    pooled = []
    for g, w in enumerate(POOL_WINDOWS):
        win = hp[:, g * POOL_GROUP:(g + 1) * POOL_GROUP]
        for k in range(1, w):
            lo = (n_hist - k) * D_POOL + g * POOL_GROUP
            win = win + pst_in_ref[:, lo:lo + POOL_GROUP]
        cnt = min(float(PAST_LEN) + 1.0, float(w))
        pooled.append(win / cnt - hp[:, g * POOL_GROUP:(g + 1) * POOL_GROUP])
    pst_ref[:, 0:(n_hist - 1) * D_POOL] = pst_in_ref[:, D_POOL:n_hist * D_POOL]
    pst_ref[:, (n_hist - 1) * D_POOL:n_hist * D_POOL] = hp

    o_ref[...] = _mix_tail(x, g_ref[...], conv_out, pooled, poolw_ref, pscale_ref[...],
                           wout_ref, lng_ref[...], lnb_ref[...])


def _mix_sample(x, sh, sc, g, cst, pst, win, cw, poolw, pscale, wout, lng, lnb):
    B, D = x.shape
    return pl.pallas_call(
        _mix_sample_kernel,
        out_shape=[
            jax.ShapeDtypeStruct((B, D), F32),
            jax.ShapeDtypeStruct(cst.shape, F32),
            jax.ShapeDtypeStruct(pst.shape, F32),
        ],
        compiler_params=pltpu.CompilerParams(vmem_limit_bytes=VMEM_LIMIT_BYTES),
        name="mix_sample",
    )(x, sh, sc, g, cst, pst, win, cw, poolw, pscale, wout, lng, lnb)


def _extract_top(s, idx, n_out):
    big = float(4 * N_EXPERTS)
    rank = jnp.full(s.shape, float(n_out), F32)
    vals = []
    for step in range(n_out):
        m = jnp.max(s, axis=0, keepdims=True)
        first = jnp.min(jnp.where(s == m, idx, big), axis=0, keepdims=True)
        sel = idx == first
        rank = jnp.where(sel, float(step), rank)
        s = jnp.where(sel, -jnp.inf, s)
        vals.append(m)
    return rank, vals


def _retrieve(s1, s2, cand_idx):
    key_idx = lax.broadcasted_iota(jnp.int32, s1.shape, 0).astype(F32)
    rank1, a = _extract_top(s1, key_idx, PEER_TOPK)
    rank2, b = _extract_top(s2, key_idx, PEER_TOPK)
    a_all = jnp.concatenate(a, axis=0)
    b_all = jnp.concatenate(b, axis=0)

    pieces = []
    for r, n in _CAND_SEGMENTS:
        pieces.append(a[r] + b_all[0:n, :])
    cand = jnp.concatenate(pieces, axis=0)
    crank, _ = _extract_top(cand, cand_idx, PEER_TOPK)
    chosen = crank < float(PEER_TOPK)

    cnt = []
    off = 0
    for r, n in _CAND_SEGMENTS:
        seg = jnp.where(chosen[off:off + n, :], 1.0, 0.0)
        cnt.append(jnp.sum(seg, axis=0, keepdims=True))
        off += n

    ea = jnp.exp(a_all - a[0])
    eb = jnp.exp(b_all - b[0])
    k_idx = lax.broadcasted_iota(jnp.int32, eb.shape, 0).astype(F32)
    z = jnp.zeros_like(a[0])
    for r in range(PEER_TOPK):
        prefix = jnp.sum(jnp.where(k_idx < cnt[r], eb, 0.0), axis=0, keepdims=True)
        z = z + ea[r:r + 1, :] * prefix
    inv_z = 1.0 / z

    cnt1 = jnp.zeros_like(s1)
    for r in range(PEER_TOPK):
        cnt1 = jnp.where(rank1 == float(r), cnt[r], cnt1)
    alpha = jnp.where(rank1 < float(PEER_TOPK), jnp.exp(s1 - a[0]) * inv_z, 0.0)
    beta = jnp.where(rank2 < float(PEER_TOPK), jnp.exp(s2 - b[0]), 0.0)
    return alpha, cnt1, beta, rank2


def _oddeven_network(n):
    pairs = []
    p = 1
    while p < n:
        k = p
        while k >= 1:
            for j in range(k % p, n - k, 2 * k):
                for i in range(min(k, n - j - k)):
                    if (i + j) // (2 * p) == (i + j + k) // (2 * p):
                        pairs.append((i + j, i + j + k))
            k //= 2
        p *= 2
    return pairs


_NET16 = _oddeven_network(PEER_TOPK)
_NET8 = _oddeven_network(SUBLANES)


def _sort_desc(v, net):
    v = list(v)
    for i, j in net:
        v[i], v[j] = jnp.maximum(v[i], v[j]), jnp.minimum(v[i], v[j])
    return v


def _bitonic_merge_desc(v):
    v = list(v)
    d = len(v) // 2
    while d >= 1:
        for i in range(len(v)):
            if (i & d) == 0:
                v[i], v[i + d] = jnp.maximum(v[i], v[i + d]), jnp.minimum(v[i], v[i + d])
        d //= 2
    return v


def _roll_sublanes(x, d):
    return pltpu.roll(x, d, 0)


def _top16_across_sublanes(v):
    for d in (4, 2, 1):
        p = [_roll_sublanes(x, d) for x in v]
        if len(v) == SUBLANES:
            v = _bitonic_merge_desc(v + p[::-1])
        else:
            v = _bitonic_merge_desc(
                [jnp.maximum(v[n], p[PEER_TOPK - 1 - n]) for n in range(PEER_TOPK)])
    return v


def _pack_rows(vals, sub):
    pick = sub & (len(vals) - 1)
    out = vals[0]
    for k in range(1, len(vals)):
        out = jnp.where(pick == k, vals[k], out)
    return out


def _sublane_sum(x):
    for d in (4, 2, 1):
        x = x + _roll_sublanes(x, d)
    return x


def _retrieve_by_threshold(s1, s2):
    n_v = N_KEYS // SUBLANES
    one = jnp.ones((SUBLANES, LANES), F32)
    zero = jnp.zeros((SUBLANES, LANES), F32)
    v1 = [s1[SUBLANES * n:SUBLANES * (n + 1), :] for n in range(n_v)]
    v2 = [s2[SUBLANES * n:SUBLANES * (n + 1), :] for n in range(n_v)]
    a = _top16_across_sublanes(_sort_desc(v1, _NET16))
    b = _top16_across_sublanes(_sort_desc(v2, _NET16))
    sub = lax.broadcasted_iota(jnp.int32, (SUBLANES, LANES), 0)

    b_lo = _pack_rows(b[0:8], sub)
    b_hi = _pack_rows(b[8:16], sub)
    b_4 = _pack_rows(b[0:4], sub)
    a_lo = _pack_rows(a[0:8], sub)
    a_hi = _pack_rows(a[8:16], sub)
    a45 = jnp.where(sub < 4, a[4], a[5])
    a67 = jnp.where(sub < 4, a[6], a[7])
    cand = [a[0] + b_lo, a[0] + b_hi, a[1] + b_lo, a[2] + b_lo, a[3] + b_lo,
            a45 + b_4, a67 + b_4, a_hi + b[0]]
    tau = _top16_across_sublanes(_sort_desc(cand, _NET8))[PEER_TOPK - 1]

    z_lo = zero
    z_hi = zero
    for k in range(PEER_TOPK):
        eb = jnp.exp(b[k] - b[0])
        z_lo = z_lo + jnp.where(a_lo + b[k] >= tau, eb, 0.0)
        z_hi = z_hi + jnp.where(a_hi + b[k] >= tau, eb, 0.0)
    z = _sublane_sum(jnp.exp(a_lo - a[0]) * z_lo + jnp.exp(a_hi - a[0]) * z_hi)
    inv_z = 1.0 / z

    alpha, cnt1, beta, rank2 = [], [], [], []
    n1 = zero
    n2 = zero
    n_pair = zero
    for n in range(n_v):
        x1 = v1[n]
        x2 = v2[n]
        in1 = x1 >= a[PEER_TOPK - 1]
        in2 = x2 >= b[PEER_TOPK - 1]
        c = zero
        r = zero
        for k in range(PEER_TOPK):
            c = c + jnp.where(x1 + b[k] >= tau, one, zero)
            r = r + jnp.where(b[k] > x2, one, zero)
        c = jnp.where(in1, c, 0.0)
        cnt1.append(c)
        rank2.append(r)
        alpha.append(jnp.where(in1, jnp.exp(x1 - a[0]) * inv_z, 0.0))
        beta.append(jnp.where(in2, jnp.exp(x2 - b[0]), 0.0))
        n1 = n1 + jnp.where(in1, one, zero)
        n2 = n2 + jnp.where(in2, one, zero)
        n_pair = n_pair + c
    want = float(PEER_TOPK)
    ok = ((_sublane_sum(n1) == want) & (_sublane_sum(n2) == want)
          & (_sublane_sum(n_pair) == want))
    cat = lambda xs: jnp.concatenate(xs, axis=0)
    return cat(alpha), cat(cnt1), cat(beta), cat(rank2), ok


def _gate_key_block(i, ib, n_tt, per_nb, at_cur, ct_cur, alpha_scr, cnt_scr, beta_scr, rho_scr):
    n_grp = N_KEYS // BF16_ROWS
    for tt in range(n_tt):
        nb, part = divmod(tt, per_nb)
        lanes = slice(part * LANES, (part + 1) * LANES)
        gate = [None] * n_grp
        for h in range(PEER_HEADS):
            al = jnp.broadcast_to(alpha_scr[h, tt, pl.ds(i, 1), :], (BF16_ROWS, LANES)).astype(BF16)
            cc = jnp.broadcast_to(cnt_scr[h, tt, pl.ds(i, 1), :], (BF16_ROWS, LANES)).astype(BF16)
            for g in range(n_grp):
                grp = slice(g * BF16_ROWS, (g + 1) * BF16_ROWS)
                hit = rho_scr[h, tt, grp, :] < cc
                term = al * jnp.where(hit, beta_scr[h, tt, grp, :], jnp.zeros((), BF16))
                gate[g] = term if gate[g] is None else gate[g] + term
        for g in range(n_grp):
            rows = slice(ib * N_KEYS + g * BF16_ROWS, ib * N_KEYS + (g + 1) * BF16_ROWS)
            t = at_cur[nb, rows, lanes]
            act = t + t * lax.erf(t)
            ct_cur[nb, rows, lanes] = act.astype(BF16) * gate[g]


def _peer_kernel(x_ref, sh_ref, sc_ref, g_ref, wqt_ref, sk_ref, cidx_ref, u_ref, vt_ref,
                 lng_ref, lnb_ref, o_ref,
                 h2s_scr, h2g_scr, ft_scr, s_scr, alpha_scr, cnt_scr, beta_scr, rho_scr,
                 at0_scr, at1_scr, ct0_scr, ct1_scr, *, tile, width):
    e = pl.program_id(2)
    n_chunks = pl.num_programs(2) - 2
    n_tt = tile // LANES
    n_nb = tile // width
    per_nb = width // LANES
    n_rb = KEYS_PER_CHUNK // n_nb
    u_rows = EXPERT_CHUNK // n_rb
    f_rows = D_MODEL // n_rb

    @pl.when(e == 0)
    def _first():
        x = x_ref[0]
        h2 = _ln_stats(x) * (1.0 + sc_ref[0]) + sh_ref[0]
        h2t = h2.T
        for nb in range(n_nb):
            blk = h2t[:, nb * width:(nb + 1) * width]
            h2s_scr[nb] = blk.astype(BF16)
            h2g_scr[nb] = (blk * math.sqrt(0.5)).astype(BF16)
        ft_scr[...] = jnp.zeros(ft_scr.shape, F32)
        ct1_scr[...] = jnp.zeros(ct1_scr.shape, BF16)
        for h in range(PEER_HEADS):
            for nb in range(n_nb):
                qt = _dot(wqt_ref[h * D_KEY:(h + 1) * D_KEY, :], h2s_scr[nb])
                for p in range(2):
                    s = _dot(sk_ref[h, p], qt[p * D_HALF:(p + 1) * D_HALF, :].astype(BF16))
                    for part in range(per_nb):
                        s_scr[p, h, nb * per_nb + part] = s[:, part * LANES:(part + 1) * LANES]

        cand_idx = cidx_ref[...]

        def topk_body(it, carry):
            h = it // n_tt
            tt = it % n_tt
            def put(alpha, cnt1, beta, rank2):
                alpha_scr[h, tt] = alpha * math.sqrt(0.5)
                cnt_scr[h, tt] = cnt1
                beta_scr[h, tt] = beta.astype(BF16)
                rho_scr[h, tt] = rank2.astype(BF16)

            alpha, cnt1, beta, rank2, ok = _retrieve_by_threshold(s_scr[0, h, tt], s_scr[1, h, tt])
            put(alpha, cnt1, beta, rank2)

            @pl.when(jnp.min(jnp.where(ok, 1.0, 0.0)) < 0.5)
            def _():
                put(*_retrieve(s_scr[0, h, tt], s_scr[1, h, tt], cand_idx))

            return carry

        lax.fori_loop(0, PEER_HEADS * n_tt, topk_body, 0)

        for nb in range(n_nb):
            at0_scr[nb] = _dot(u_ref[...], h2g_scr[nb])

    def steady(at_new, at_cur, ct_cur, ct_old):
        for ib in range(KEYS_PER_CHUNK):
            nb, rb = divmod(ib, n_rb)
            urows = slice(rb * u_rows, (rb + 1) * u_rows)
            frows = slice(rb * f_rows, (rb + 1) * f_rows)
            at_new[nb, urows, :] = _dot(u_ref[urows, :], h2g_scr[nb])
            _gate_key_block((e - 1) * KEYS_PER_CHUNK + ib, ib, n_tt, per_nb, at_cur, ct_cur,
                            alpha_scr, cnt_scr, beta_scr, rho_scr)
            ft_scr[nb, frows, :] += _dot(vt_ref[frows, :], ct_old[nb])

    in_steady = jnp.logical_and(e >= 1, e <= n_chunks)

    @pl.when(jnp.logical_and(in_steady, e % 2 == 1))
    def _odd():
        steady(at1_scr, at0_scr, ct0_scr, ct1_scr)

    @pl.when(jnp.logical_and(in_steady, e % 2 == 0))
    def _even():
        steady(at0_scr, at1_scr, ct1_scr, ct0_scr)

    @pl.when(e == n_chunks + 1)
    def _last():
        for nb in range(n_nb):
            f = (ft_scr[nb] + _dot(vt_ref[...], ct1_scr[nb])).T
            rows = slice(nb * width, (nb + 1) * width)
            g = g_ref[0] if g_ref.shape[1] == 1 else g_ref[0, rows, :]
            y = ALPHA * x_ref[0, rows, :] + g * f
            o_ref[0, rows, :] = _ln_stats(y) * lng_ref[...] + lnb_ref[...]


def _peer(x, sh, sc, g, wqt, sk, cand_idx, u, vt, lng, lnb, *, tile):
    B, S, D = x.shape
    nt = S // tile
    ne = N_EXPERTS // EXPERT_CHUNK
    assert ne % 2 == 0
    n_tt = tile // LANES
    width = min(tile, MXU_COLS)
    n_nb = tile // width
    R = sh.shape[1]
    if R == 1:
        mod_spec = pl.BlockSpec((1, 1, D), lambda b, t, e: (b, 0, 0))
    else:
        mod_spec = pl.BlockSpec((1, tile, D), lambda b, t, e: (b, t, 0))
    kern = functools.partial(_peer_kernel, tile=tile, width=width)
    return pl.pallas_call(
        kern,
        grid=(B, nt, ne + 2),
        in_specs=[
            pl.BlockSpec((1, tile, D), lambda b, t, e: (b, t, 0)),
            mod_spec, mod_spec, mod_spec,
            pl.BlockSpec((PEER_HEADS * D_KEY, D), lambda b, t, e: (0, 0)),
            pl.BlockSpec((PEER_HEADS, 2, N_KEYS, D_HALF), lambda b, t, e: (0, 0, 0, 0)),
            pl.BlockSpec((N_CAND, LANES), lambda b, t, e: (0, 0)),
            pl.BlockSpec((EXPERT_CHUNK, D), lambda b, t, e: (jnp.minimum(e, ne - 1), 0)),
            pl.BlockSpec((D, EXPERT_CHUNK), lambda b, t, e: (0, jnp.clip(e - 2, 0, ne - 1))),
            pl.BlockSpec((1, D), lambda b, t, e: (0, 0)),
            pl.BlockSpec((1, D), lambda b, t, e: (0, 0)),
        ],
        out_specs=pl.BlockSpec((1, tile, D), lambda b, t, e: (b, t, 0)),
        out_shape=jax.ShapeDtypeStruct((B, S, D), F32),
        scratch_shapes=[
            pltpu.VMEM((n_nb, D, width), BF16),
            pltpu.VMEM((n_nb, D, width), BF16),
            pltpu.VMEM((n_nb, D, width), F32),
            pltpu.VMEM((2, PEER_HEADS, n_tt, N_KEYS, LANES), F32),
            pltpu.VMEM((PEER_HEADS, n_tt, N_KEYS, LANES), F32),
            pltpu.VMEM((PEER_HEADS, n_tt, N_KEYS, LANES), F32),
            pltpu.VMEM((PEER_HEADS, n_tt, N_KEYS, LANES), BF16),
            pltpu.VMEM((PEER_HEADS, n_tt, N_KEYS, LANES), BF16),
            pltpu.VMEM((n_nb, EXPERT_CHUNK, width), F32),
            pltpu.VMEM((n_nb, EXPERT_CHUNK, width), F32),
            pltpu.VMEM((n_nb, EXPERT_CHUNK, width), BF16),
            pltpu.VMEM((n_nb, EXPERT_CHUNK, width), BF16),
        ],
        compiler_params=pltpu.CompilerParams(
            dimension_semantics=("arbitrary", "arbitrary", "arbitrary"),
            vmem_limit_bytes=VMEM_LIMIT_BYTES),
        name="peer",
    )(x, sh, sc, g, wqt, sk, cand_idx, u, vt, lng, lnb)


def _cand_index_table():
    idx = []
    for r, n in _CAND_SEGMENTS:
        idx.extend(r * PEER_TOPK + k for k in range(n))
    col = np.asarray(idx, np.float32)[:, None]
    return jnp.asarray(np.broadcast_to(col, (N_CAND, LANES)).copy())


def kernel(x_prompt, x_sample, state_conv, state_pool, c_prompt, c_sample, w_ada, b_ada, w_in, conv_w, pool_w, pool_scale, w_out, ln1_g, ln1_b, w_q, sub_keys, u_experts, v_experts, ln2_g, ln2_b):
    B, S, D = x_prompt.shape
    Bs = x_sample.shape[0]

    w_in_b = w_in.astype(BF16)
    w_out_b = w_out.astype(BF16)
    pool_w_b = pool_w.astype(BF16)
    wqt_b = jnp.swapaxes(w_q, 1, 2).astype(BF16)
    sk_b = sub_keys.astype(BF16)
    u_b = u_experts.astype(BF16)
    vt_b = jnp.swapaxes(v_experts, 1, 2).astype(BF16)
    cand_idx = _cand_index_table()

    c_all = jnp.concatenate([c_prompt, c_sample], axis=0)
    mod = _ada_modulation(c_all, w_ada, b_ada)

    xp = x_prompt
    xs = x_sample.reshape(Bs, D)
    conv_p, pool_p, conv_s, pool_s = [], [], [], []
    for l in range(DEPTH):
        mp = [mod[l, :B, k * D:(k + 1) * D].reshape(B, 1, D) for k in range(6)]
        ms = [mod[l, B:, k * D:(k + 1) * D] for k in range(6)]
        row = lambda a: a[l].reshape(1, -1)

        xp, cst, pst = _mix_prompt(xp, mp[0], mp[1], mp[2], w_in_b[l], conv_w[l], pool_w_b[l],
                                   row(pool_scale), w_out_b[l], row(ln1_g), row(ln1_b))
        conv_p.append(cst)
        pool_p.append(pst)
        xp = _peer(xp, mp[3], mp[4], mp[5], wqt_b[l], sk_b[l], cand_idx, u_b[l], vt_b[l],
                   row(ln2_g), row(ln2_b), tile=PEER_TILE)

        xs, cst, pst = _mix_sample(
            xs, ms[0], ms[1], ms[2],
            state_conv[l].reshape(Bs, (CONV_W - 1) * D_CONV),
            state_pool[l].reshape(Bs, (POOL_MAX - 1) * D_POOL),
            w_in_b[l], conv_w[l], pool_w_b[l], row(pool_scale), w_out_b[l],
            row(ln1_g), row(ln1_b))
        conv_s.append(cst.reshape(Bs, CONV_W - 1, D_CONV))
        pool_s.append(pst.reshape(Bs, POOL_MAX - 1, D_POOL))
        xs = _peer(xs.reshape(1, Bs, D), ms[3].reshape(1, Bs, D), ms[4].reshape(1, Bs, D),
                   ms[5].reshape(1, Bs, D), wqt_b[l], sk_b[l], cand_idx, u_b[l], vt_b[l],
                   row(ln2_g), row(ln2_b), tile=Bs).reshape(Bs, D)

    return (xp, xs.reshape(Bs, 1, D), jnp.stack(conv_p), jnp.stack(pool_p),
            jnp.stack(conv_s), jnp.stack(pool_s))
```

```python
import functools
import math

import jax
import jax.numpy as jnp
import numpy as np
from jax import lax
from jax.experimental import pallas as pl
from jax.experimental.pallas import tpu as pltpu

F32 = jnp.float32
BF16 = jnp.bfloat16

D_MODEL = 1024
DEPTH = 2
D_CONV = D_MODEL // 2
D_POOL = D_MODEL - D_CONV
CONV_W = 3
POOL_WINDOWS = (2, 4, 8, 16)
POOL_GROUP = D_POOL // len(POOL_WINDOWS)
POOL_MAX = max(POOL_WINDOWS)
D_IN = 3 * D_CONV + D_POOL
PEER_HEADS = 8
N_KEYS = 128
N_EXPERTS = N_KEYS * N_KEYS
PEER_TOPK = 16
D_KEY = 256
D_HALF = D_KEY // 2
PAST_LEN = 16384
ALPHA = (2 * DEPTH) ** 0.25
LN_EPS = 1e-5

LANES = 128
SUBLANES = 8
BF16_ROWS = 16
MXU_COLS = 256
VMEM_LIMIT_BYTES = 56 * 1024 * 1024

MIX_TILE = 512
PEER_TILE = 512
EXPERT_CHUNK = 2048
DOT_ROWS = 512
KEYS_PER_CHUNK = EXPERT_CHUNK // N_KEYS
KEYS_PER_STEP = 1
ADA_COLS = 1536

_CAND_SEGMENTS = ((0, 16), (1, 8), (2, 8), (3, 8), (4, 4), (5, 4), (6, 4), (7, 4)) + tuple(
    (r, 1) for r in range(8, 16))
N_CAND = sum(n for _, n in _CAND_SEGMENTS)
assert N_CAND == 64


def _ln_stats(x):
    mu = jnp.mean(x, axis=-1, keepdims=True)
    xc = x - mu
    var = jnp.mean(xc * xc, axis=-1, keepdims=True)
    return xc * lax.rsqrt(var + LN_EPS)


def _gelu(x):
    return 0.5 * x * (1.0 + lax.erf(x * (1.0 / math.sqrt(2.0))))


def _dot(a, b):
    return jnp.dot(a, b, preferred_element_type=F32)


def _ada_kernel(c_ref, w_ref, b_ref, o_ref):
    c = c_ref[...]
    act = (c * jax.nn.sigmoid(c)).astype(BF16)
    o_ref[0] = _dot(act, w_ref[0].astype(BF16)) + b_ref[0]


def _ada_modulation(c_all, w_ada, b_ada):
    n = c_all.shape[0]
    n_cols = 6 * D_MODEL
    return pl.pallas_call(
        _ada_kernel,
        grid=(DEPTH, n_cols // ADA_COLS),
        in_specs=[
            pl.BlockSpec((n, D_MODEL), lambda l, j: (0, 0)),
            pl.BlockSpec((1, D_MODEL, ADA_COLS), lambda l, j: (l, 0, j)),
            pl.BlockSpec((1, 1, ADA_COLS), lambda l, j: (l, 0, j)),
        ],
        out_specs=pl.BlockSpec((1, n, ADA_COLS), lambda l, j: (l, 0, j)),
        out_shape=jax.ShapeDtypeStruct((DEPTH, n, n_cols), F32),
        compiler_params=pltpu.CompilerParams(
            dimension_semantics=("arbitrary", "arbitrary"),
            vmem_limit_bytes=VMEM_LIMIT_BYTES),
        name="ada_modulation",
    )(c_all, w_ada, b_ada.reshape(DEPTH, 1, n_cols))


def _mix_tail(x, g1, conv_out, pooled, poolw_ref, pscale, wout_ref, lng, lnb):
    mixed = [
        _dot(pooled[g].astype(BF16), poolw_ref[g]) for g in range(len(POOL_WINDOWS))
    ]
    pool_out = jnp.concatenate(mixed, axis=-1) * pscale
    cat = jnp.concatenate([conv_out, pool_out], axis=-1).astype(BF16)
    mix = _dot(cat, wout_ref[...])
    return _ln_stats(ALPHA * x + g1 * mix) * lng + lnb


def _mix_prompt_kernel(x_ref, sh_ref, sc_ref, g_ref, win_ref, cw_ref, poolw_ref,
                       pscale_ref, wout_ref, lng_ref, lnb_ref,
                       o_ref, cst_ref, pst_ref, ue_scr, pe_scr):
    j = pl.program_id(1)
    nt = pl.num_programs(1)
    T = MIX_TILE
    HU = SUBLANES
    HP = POOL_MAX

    @pl.when(j == 0)
    def _():
        ue_scr[0:HU, :] = jnp.zeros((HU, D_CONV), F32)
        pe_scr[0:HP, :] = jnp.zeros((HP, D_POOL), F32)

    x = x_ref[0]
    h = _ln_stats(x) * (1.0 + sc_ref[0]) + sh_ref[0]
    z = _dot(h.astype(BF16), win_ref[...])
    bg = z[:, 0:D_CONV]
    cg = z[:, D_CONV:2 * D_CONV]
    hc = z[:, 2 * D_CONV:3 * D_CONV]
    hp = z[:, 3 * D_CONV:]

    u = cg * hc
    ue_scr[HU:HU + T, :] = u
    pe_scr[HP:HP + T, :] = hp

    cw = cw_ref[...]
    y = (ue_scr[HU - 2:HU - 2 + T, :] * cw[0:1, :]
         + ue_scr[HU - 1:HU - 1 + T, :] * cw[1:2, :]
         + u * cw[2:3, :])
    conv_out = bg * y

    pos = (lax.broadcasted_iota(jnp.int32, (T, POOL_GROUP), 0) + j * T).astype(F32)
    pooled = []
    for g, w in enumerate(POOL_WINDOWS):
        lanes = slice(g * POOL_GROUP, (g + 1) * POOL_GROUP)
        win = hp[:, lanes]
        for k in range(1, w):
            win = win + pe_scr[HP - k:HP - k + T, lanes]
        cnt = jnp.minimum(pos + 1.0, float(w))
        pooled.append(win / cnt - hp[:, lanes])

    o_ref[0] = _mix_tail(x, g_ref[0], conv_out, pooled, poolw_ref, pscale_ref[...],
                         wout_ref, lng_ref[...], lnb_ref[...])

    @pl.when(j == nt - 1)
    def _():
        cst_ref[0] = ue_scr[HU + T - (CONV_W - 1):HU + T, :]
        pst_ref[0] = pe_scr[HP + T - (POOL_MAX - 1):HP + T, :]

    @pl.when(j < nt - 1)
    def _():
        ue_scr[0:HU, :] = ue_scr[T:T + HU, :]
        pe_scr[0:HP, :] = pe_scr[T:T + HP, :]


def _mix_prompt(x, sh, sc, g, win, cw, poolw, pscale, wout, lng, lnb):
    B, S, D = x.shape
    nt = S // MIX_TILE
    row = lambda b, j: (b, 0, 0)
    full2 = lambda b, j: (0, 0)
    full3 = lambda b, j: (0, 0, 0)
    return pl.pallas_call(
        _mix_prompt_kernel,
        grid=(B, nt),
        in_specs=[
            pl.BlockSpec((1, MIX_TILE, D), lambda b, j: (b, j, 0)),
            pl.BlockSpec((1, 1, D), row),
            pl.BlockSpec((1, 1, D), row),
            pl.BlockSpec((1, 1, D), row),
            pl.BlockSpec((D, D_IN), full2),
            pl.BlockSpec((CONV_W, D_CONV), full2),
            pl.BlockSpec((len(POOL_WINDOWS), POOL_GROUP, POOL_GROUP), full3),
            pl.BlockSpec((1, D_POOL), full2),
            pl.BlockSpec((D, D), full2),
            pl.BlockSpec((1, D), full2),
            pl.BlockSpec((1, D), full2),
        ],
        out_specs=[
            pl.BlockSpec((1, MIX_TILE, D), lambda b, j: (b, j, 0)),
            pl.BlockSpec((1, CONV_W - 1, D_CONV), row),
            pl.BlockSpec((1, POOL_MAX - 1, D_POOL), row),
        ],
        out_shape=[
            jax.ShapeDtypeStruct((B, S, D), F32),
            jax.ShapeDtypeStruct((B, CONV_W - 1, D_CONV), F32),
            jax.ShapeDtypeStruct((B, POOL_MAX - 1, D_POOL), F32),
        ],
        scratch_shapes=[
            pltpu.VMEM((MIX_TILE + SUBLANES, D_CONV), F32),
            pltpu.VMEM((MIX_TILE + POOL_MAX, D_POOL), F32),
        ],
        compiler_params=pltpu.CompilerParams(
            dimension_semantics=("arbitrary", "arbitrary"),
            vmem_limit_bytes=VMEM_LIMIT_BYTES),
        name="mix_prompt",
    )(x, sh, sc, g, win, cw, poolw, pscale, wout, lng, lnb)


def _mix_sample_kernel(x_ref, sh_ref, sc_ref, g_ref, cst_in_ref, pst_in_ref, win_ref,
                       cw_ref, poolw_ref, pscale_ref, wout_ref, lng_ref, lnb_ref,
                       o_ref, cst_ref, pst_ref):
    x = x_ref[...]
    h = _ln_stats(x) * (1.0 + sc_ref[...]) + sh_ref[...]
    z = _dot(h.astype(BF16), win_ref[...])
    bg = z[:, 0:D_CONV]
    cg = z[:, D_CONV:2 * D_CONV]
    hc = z[:, 2 * D_CONV:3 * D_CONV]
    hp = z[:, 3 * D_CONV:]

    u = cg * hc
    cw = cw_ref[...]
    prev2 = cst_in_ref[:, 0:D_CONV]
    prev1 = cst_in_ref[:, D_CONV:2 * D_CONV]
    conv_out = bg * (prev2 * cw[0:1, :] + prev1 * cw[1:2, :] + u * cw[2:3, :])
    cst_ref[:, 0:D_CONV] = prev1
    cst_ref[:, D_CONV:2 * D_CONV] = u

    n_hist = POOL_MAX - 1
    pooled = []
    for g, w in enumerate(POOL_WINDOWS):
        win = hp[:, g * POOL_GROUP:(g + 1) * POOL_GROUP]
        for k in range(1, w):
            lo = (n_hist - k) * D_POOL + g * POOL_GROUP
            win = win + pst_in_ref[:, lo:lo + POOL_GROUP]
        cnt = min(float(PAST_LEN) + 1.0, float(w))
        pooled.append(win / cnt - hp[:, g * POOL_GROUP:(g + 1) * POOL_GROUP])
    pst_ref[:, 0:(n_hist - 1) * D_POOL] = pst_in_ref[:, D_POOL:n_hist * D_POOL]
    pst_ref[:, (n_hist - 1) * D_POOL:n_hist * D_POOL] = hp

    o_ref[...] = _mix_tail(x, g_ref[...], conv_out, pooled, poolw_ref, pscale_ref[...],
                           wout_ref, lng_ref[...], lnb_ref[...])


def _mix_sample(x, sh, sc, g, cst, pst, win, cw, poolw, pscale, wout, lng, lnb):
    B, D = x.shape
    return pl.pallas_call(
        _mix_sample_kernel,
        out_shape=[
            jax.ShapeDtypeStruct((B, D), F32),
            jax.ShapeDtypeStruct(cst.shape, F32),
            jax.ShapeDtypeStruct(pst.shape, F32),
        ],
        compiler_params=pltpu.CompilerParams(vmem_limit_bytes=VMEM_LIMIT_BYTES),
        name="mix_sample",
    )(x, sh, sc, g, cst, pst, win, cw, poolw, pscale, wout, lng, lnb)


def _extract_top(s, idx, n_out):
    big = float(4 * N_EXPERTS)
    rank = jnp.full(s.shape, float(n_out), F32)
    vals = []
    for step in range(n_out):
        m = jnp.max(s, axis=0, keepdims=True)
        first = jnp.min(jnp.where(s == m, idx, big), axis=0, keepdims=True)
        sel = idx == first
        rank = jnp.where(sel, float(step), rank)
        s = jnp.where(sel, -jnp.inf, s)
        vals.append(m)
    return rank, vals


def _retrieve(s1, s2, cand_idx):
    key_idx = lax.broadcasted_iota(jnp.int32, s1.shape, 0).astype(F32)
    rank1, a = _extract_top(s1, key_idx, PEER_TOPK)
    rank2, b = _extract_top(s2, key_idx, PEER_TOPK)
    a_all = jnp.concatenate(a, axis=0)
    b_all = jnp.concatenate(b, axis=0)

    pieces = []
    for r, n in _CAND_SEGMENTS:
        pieces.append(a[r] + b_all[0:n, :])
    cand = jnp.concatenate(pieces, axis=0)
    crank, _ = _extract_top(cand, cand_idx, PEER_TOPK)
    chosen = crank < float(PEER_TOPK)

    cnt = []
    off = 0
    for r, n in _CAND_SEGMENTS:
        seg = jnp.where(chosen[off:off + n, :], 1.0, 0.0)
        cnt.append(jnp.sum(seg, axis=0, keepdims=True))
        off += n

    ea = jnp.exp(a_all - a[0])
    eb = jnp.exp(b_all - b[0])
    k_idx = lax.broadcasted_iota(jnp.int32, eb.shape, 0).astype(F32)
    z = jnp.zeros_like(a[0])
    for r in range(PEER_TOPK):
        prefix = jnp.sum(jnp.where(k_idx < cnt[r], eb, 0.0), axis=0, keepdims=True)
        z = z + ea[r:r + 1, :] * prefix
    inv_z = 1.0 / z

    cnt1 = jnp.zeros_like(s1)
    for r in range(PEER_TOPK):
        cnt1 = jnp.where(rank1 == float(r), cnt[r], cnt1)
    alpha = jnp.where(rank1 < float(PEER_TOPK), jnp.exp(s1 - a[0]) * inv_z, 0.0)
    beta = jnp.where(rank2 < float(PEER_TOPK), jnp.exp(s2 - b[0]), 0.0)
    return alpha, cnt1, beta, rank2


def _oddeven_network(n):
    pairs = []
    p = 1
    while p < n:
        k = p
        while k >= 1:
            for j in range(k % p, n - k, 2 * k):
                for i in range(min(k, n - j - k)):
                    if (i + j) // (2 * p) == (i + j + k) // (2 * p):
                        pairs.append((i + j, i + j + k))
            k //= 2
        p *= 2
    return pairs


_NET16 = _oddeven_network(PEER_TOPK)
_NET8 = _oddeven_network(SUBLANES)


def _sort_desc(v, net):
    v = list(v)
    for i, j in net:
        v[i], v[j] = jnp.maximum(v[i], v[j]), jnp.minimum(v[i], v[j])
    return v


def _bitonic_merge_desc(v):
    v = list(v)
    d = len(v) // 2
    while d >= 1:
        for i in range(len(v)):
            if (i & d) == 0:
                v[i], v[i + d] = jnp.maximum(v[i], v[i + d]), jnp.minimum(v[i], v[i + d])
        d //= 2
    return v


def _roll_sublanes(x, d):
    return pltpu.roll(x, d, 0)


def _top16_across_sublanes(v):
    for d in (4, 2, 1):
        p = [_roll_sublanes(x, d) for x in v]
        if len(v) == SUBLANES:
            v = _bitonic_merge_desc(v + p[::-1])
        else:
            v = _bitonic_merge_desc(
                [jnp.maximum(v[n], p[PEER_TOPK - 1 - n]) for n in range(PEER_TOPK)])
    return v


def _pack_rows(vals, sub):
    pick = sub & (len(vals) - 1)
    out = vals[0]
    for k in range(1, len(vals)):
        out = jnp.where(pick == k, vals[k], out)
    return out


def _sublane_sum(x):
    for d in (4, 2, 1):
        x = x + _roll_sublanes(x, d)
    return x


def _retrieve_by_threshold(s1, s2):
    n_v = N_KEYS // SUBLANES
    one = jnp.ones((SUBLANES, LANES), F32)
    zero = jnp.zeros((SUBLANES, LANES), F32)
    v1 = [s1[SUBLANES * n:SUBLANES * (n + 1), :] for n in range(n_v)]
    v2 = [s2[SUBLANES * n:SUBLANES * (n + 1), :] for n in range(n_v)]
    a = _top16_across_sublanes(_sort_desc(v1, _NET16))
    b = _top16_across_sublanes(_sort_desc(v2, _NET16))
    sub = lax.broadcasted_iota(jnp.int32, (SUBLANES, LANES), 0)

    b_lo = _pack_rows(b[0:8], sub)
    b_hi = _pack_rows(b[8:16], sub)
    b_4 = _pack_rows(b[0:4], sub)
    a_lo = _pack_rows(a[0:8], sub)
    a_hi = _pack_rows(a[8:16], sub)
    a45 = jnp.where(sub < 4, a[4], a[5])
    a67 = jnp.where(sub < 4, a[6], a[7])
    cand = [a[0] + b_lo, a[0] + b_hi, a[1] + b_lo, a[2] + b_lo, a[3] + b_lo,
            a45 + b_4, a67 + b_4, a_hi + b[0]]
    tau = _top16_across_sublanes(_sort_desc(cand, _NET8))[PEER_TOPK - 1]

    z_lo = zero
    z_hi = zero
    for k in range(PEER_TOPK):
        eb = jnp.exp(b[k] - b[0])
        z_lo = z_lo + jnp.where(a_lo + b[k] >= tau, eb, 0.0)
        z_hi = z_hi + jnp.where(a_hi + b[k] >= tau, eb, 0.0)
    z = _sublane_sum(jnp.exp(a_lo - a[0]) * z_lo + jnp.exp(a_hi - a[0]) * z_hi)
    inv_z = 1.0 / z

    alpha, cnt1, beta, rank2 = [], [], [], []
    n1 = zero
    n2 = zero
    n_pair = zero
    for n in range(n_v):
        x1 = v1[n]
        x2 = v2[n]
        in1 = x1 >= a[PEER_TOPK - 1]
        in2 = x2 >= b[PEER_TOPK - 1]
        c = zero
        r = zero
        for k in range(PEER_TOPK):
            c = c + jnp.where(x1 + b[k] >= tau, one, zero)
            r = r + jnp.where(b[k] > x2, one, zero)
        c = jnp.where(in1, c, 0.0)
        cnt1.append(c)
        rank2.append(r)
        alpha.append(jnp.where(in1, jnp.exp(x1 - a[0]) * inv_z, 0.0))
        beta.append(jnp.where(in2, jnp.exp(x2 - b[0]), 0.0))
        n1 = n1 + jnp.where(in1, one, zero)
        n2 = n2 + jnp.where(in2, one, zero)
        n_pair = n_pair + c
    want = float(PEER_TOPK)
    ok = ((_sublane_sum(n1) == want) & (_sublane_sum(n2) == want)
          & (_sublane_sum(n_pair) == want))
    cat = lambda xs: jnp.concatenate(xs, axis=0)
    return cat(alpha), cat(cnt1), cat(beta), cat(rank2), ok


def _gate_key_block(i, ib, n_tt, per_nb, act_scr, ct_scr, alpha_scr, cnt_scr, beta_scr, rho_scr):
    n_grp = N_KEYS // BF16_ROWS
    zero = jnp.zeros((), BF16)
    for tt in range(n_tt):
        nb, part = divmod(tt, per_nb)
        lanes = slice(part * LANES, (part + 1) * LANES)
        gate = [[None] * n_grp for _ in range(KEYS_PER_STEP)]
        for h in range(PEER_HEADS):
            al = [jnp.broadcast_to(alpha_scr[h, tt, pl.ds(i + d, 1), :],
                                   (BF16_ROWS, LANES)).astype(BF16) for d in range(KEYS_PER_STEP)]
            cc = [jnp.broadcast_to(cnt_scr[h, tt, pl.ds(i + d, 1), :],
                                   (BF16_ROWS, LANES)).astype(BF16) for d in range(KEYS_PER_STEP)]
            for g in range(n_grp):
                grp = slice(g * BF16_ROWS, (g + 1) * BF16_ROWS)
                rho = rho_scr[h, tt, grp, :]
                beta = beta_scr[h, tt, grp, :]
                for d in range(KEYS_PER_STEP):
                    term = al[d] * jnp.where(rho < cc[d], beta, zero)
                    gate[d][g] = term if gate[d][g] is None else gate[d][g] + term
        for d in range(KEYS_PER_STEP):
            for g in range(n_grp):
                lo = (ib + d) * N_KEYS + g * BF16_ROWS
                rows = slice(lo, lo + BF16_ROWS)
                ct_scr[nb, rows, lanes] = act_scr[nb, rows, lanes] * gate[d][g]


def _peer_kernel(x_ref, sh_ref, sc_ref, g_ref, wqt_ref, sk_ref, cidx_ref, u_ref, vt_ref,
                 lng_ref, lnb_ref, o_ref,
                 h2s_scr, h2g_scr, ft_scr, s_scr, alpha_scr, cnt_scr, beta_scr, rho_scr,
                 act_scr, ct_scr, *, tile, width):
    e = pl.program_id(2)
    n_chunks = pl.num_programs(2) - 1
    n_tt = tile // LANES
    n_nb = tile // width
    per_nb = width // LANES

    def activations():
        for nb in range(n_nb):
            for r0 in range(0, EXPERT_CHUNK, DOT_ROWS):
                t = _dot(u_ref[r0:r0 + DOT_ROWS, :], h2g_scr[nb])
                act_scr[nb, r0:r0 + DOT_ROWS, :] = (t + t * lax.erf(t)).astype(BF16)

    def accumulate():
        for nb in range(n_nb):
            for r0 in range(0, D_MODEL, DOT_ROWS):
                ft_scr[nb, r0:r0 + DOT_ROWS, :] += _dot(vt_ref[r0:r0 + DOT_ROWS, :], ct_scr[nb])

    @pl.when(e == 0)
    def _first():
        x = x_ref[0]
        h2 = _ln_stats(x) * (1.0 + sc_ref[0]) + sh_ref[0]
        h2t = h2.T
        for nb in range(n_nb):
            blk = h2t[:, nb * width:(nb + 1) * width]
            h2s_scr[nb] = blk.astype(BF16)
            h2g_scr[nb] = (blk * math.sqrt(0.5)).astype(BF16)
        ft_scr[...] = jnp.zeros(ft_scr.shape, F32)
        for h in range(PEER_HEADS):
            for nb in range(n_nb):
                qt = _dot(wqt_ref[h * D_KEY:(h + 1) * D_KEY, :], h2s_scr[nb])
                for p in range(2):
                    s = _dot(sk_ref[h, p], qt[p * D_HALF:(p + 1) * D_HALF, :].astype(BF16))
                    for part in range(per_nb):
                        s_scr[p, h, nb * per_nb + part] = s[:, part * LANES:(part + 1) * LANES]

        cand_idx = cidx_ref[...]

        def topk_body(it, carry):
            h = it // n_tt
            tt = it % n_tt
            def put(alpha, cnt1, beta, rank2):
                alpha_scr[h, tt] = alpha * math.sqrt(0.5)
                cnt_scr[h, tt] = cnt1
                beta_scr[h, tt] = beta.astype(BF16)
                rho_scr[h, tt] = rank2.astype(BF16)

            alpha, cnt1, beta, rank2, ok = _retrieve_by_threshold(s_scr[0, h, tt], s_scr[1, h, tt])
            put(alpha, cnt1, beta, rank2)

            @pl.when(jnp.min(jnp.where(ok, 1.0, 0.0)) < 0.5)
            def _():
                put(*_retrieve(s_scr[0, h, tt], s_scr[1, h, tt], cand_idx))

            return carry

        lax.fori_loop(0, PEER_HEADS * n_tt, topk_body, 0)
        activations()

    @pl.when(jnp.logical_and(e > 0, e < n_chunks))
    def _mxu():
        accumulate()
        activations()

    @pl.when(e < n_chunks)
    def _gate():
        for ib in range(0, KEYS_PER_CHUNK, KEYS_PER_STEP):
            _gate_key_block(e * KEYS_PER_CHUNK + ib, ib, n_tt, per_nb, act_scr, ct_scr,
                            alpha_scr, cnt_scr, beta_scr, rho_scr)

    @pl.when(e == n_chunks)
    def _last():
        accumulate()
        for nb in range(n_nb):
            f = ft_scr[nb].T
            rows = slice(nb * width, (nb + 1) * width)
            g = g_ref[0] if g_ref.shape[1] == 1 else g_ref[0, rows, :]
            y = ALPHA * x_ref[0, rows, :] + g * f
            o_ref[0, rows, :] = _ln_stats(y) * lng_ref[...] + lnb_ref[...]


def _peer(x, sh, sc, g, wqt, sk, cand_idx, u, vt, lng, lnb, *, tile):
    B, S, D = x.shape
    nt = S // tile
    ne = N_EXPERTS // EXPERT_CHUNK
    n_tt = tile // LANES
    width = min(tile, MXU_COLS)
    n_nb = tile // width
    R = sh.shape[1]
    if R == 1:
        mod_spec = pl.BlockSpec((1, 1, D), lambda b, t, e: (b, 0, 0))
    else:
        mod_spec = pl.BlockSpec((1, tile, D), lambda b, t, e: (b, t, 0))
    kern = functools.partial(_peer_kernel, tile=tile, width=width)
    return pl.pallas_call(
        kern,
        grid=(B, nt, ne + 1),
        in_specs=[
            pl.BlockSpec((1, tile, D), lambda b, t, e: (b, t, 0)),
            mod_spec, mod_spec, mod_spec,
            pl.BlockSpec((PEER_HEADS * D_KEY, D), lambda b, t, e: (0, 0)),
            pl.BlockSpec((PEER_HEADS, 2, N_KEYS, D_HALF), lambda b, t, e: (0, 0, 0, 0)),
            pl.BlockSpec((N_CAND, LANES), lambda b, t, e: (0, 0)),
            pl.BlockSpec((EXPERT_CHUNK, D), lambda b, t, e: (jnp.minimum(e, ne - 1), 0)),
            pl.BlockSpec((D, EXPERT_CHUNK), lambda b, t, e: (0, jnp.clip(e - 1, 0, ne - 1))),
            pl.BlockSpec((1, D), lambda b, t, e: (0, 0)),
            pl.BlockSpec((1, D), lambda b, t, e: (0, 0)),
        ],
        out_specs=pl.BlockSpec((1, tile, D), lambda b, t, e: (b, t, 0)),
        out_shape=jax.ShapeDtypeStruct((B, S, D), F32),
        scratch_shapes=[
            pltpu.VMEM((n_nb, D, width), BF16),
            pltpu.VMEM((n_nb, D, width), BF16),
            pltpu.VMEM((n_nb, D, width), F32),
            pltpu.VMEM((2, PEER_HEADS, n_tt, N_KEYS, LANES), F32),
            pltpu.VMEM((PEER_HEADS, n_tt, N_KEYS, LANES), F32),
            pltpu.VMEM((PEER_HEADS, n_tt, N_KEYS, LANES), F32),
            pltpu.VMEM((PEER_HEADS, n_tt, N_KEYS, LANES), BF16),
            pltpu.VMEM((PEER_HEADS, n_tt, N_KEYS, LANES), BF16),
            pltpu.VMEM((n_nb, EXPERT_CHUNK, width), BF16),
            pltpu.VMEM((n_nb, EXPERT_CHUNK, width), BF16),
        ],
        compiler_params=pltpu.CompilerParams(
            dimension_semantics=("arbitrary", "arbitrary", "arbitrary"),
            vmem_limit_bytes=VMEM_LIMIT_BYTES),
        name="peer",
    )(x, sh, sc, g, wqt, sk, cand_idx, u, vt, lng, lnb)


def _cand_index_table():
    idx = []
    for r, n in _CAND_SEGMENTS:
        idx.extend(r * PEER_TOPK + k for k in range(n))
    col = np.asarray(idx, np.float32)[:, None]
    return jnp.asarray(np.broadcast_to(col, (N_CAND, LANES)).copy())


def kernel(x_prompt, x_sample, state_conv, state_pool, c_prompt, c_sample, w_ada, b_ada, w_in, conv_w, pool_w, pool_scale, w_out, ln1_g, ln1_b, w_q, sub_keys, u_experts, v_experts, ln2_g, ln2_b):
    B, S, D = x_prompt.shape
    Bs = x_sample.shape[0]

    w_in_b = w_in.astype(BF16)
    w_out_b = w_out.astype(BF16)
    pool_w_b = pool_w.astype(BF16)
    wqt_b = jnp.swapaxes(w_q, 1, 2).astype(BF16)
    sk_b = sub_keys.astype(BF16)
    u_b = u_experts.astype(BF16)
    vt_b = jnp.swapaxes(v_experts, 1, 2).astype(BF16)
    cand_idx = _cand_index_table()

    c_all = jnp.concatenate([c_prompt, c_sample], axis=0)
    mod = _ada_modulation(c_all, w_ada, b_ada)

    xp = x_prompt
    xs = x_sample.reshape(Bs, D)
    conv_p, pool_p, conv_s, pool_s = [], [], [], []
    for l in range(DEPTH):
        mp = [mod[l, :B, k * D:(k + 1) * D].reshape(B, 1, D) for k in range(6)]
        ms = [mod[l, B:, k * D:(k + 1) * D] for k in range(6)]
        row = lambda a: a[l].reshape(1, -1)

        xp, cst, pst = _mix_prompt(xp, mp[0], mp[1], mp[2], w_in_b[l], conv_w[l], pool_w_b[l],
                                   row(pool_scale), w_out_b[l], row(ln1_g), row(ln1_b))
        conv_p.append(cst)
        pool_p.append(pst)
        xp = _peer(xp, mp[3], mp[4], mp[5], wqt_b[l], sk_b[l], cand_idx, u_b[l], vt_b[l],
                   row(ln2_g), row(ln2_b), tile=PEER_TILE)

        xs, cst, pst = _mix_sample(
            xs, ms[0], ms[1], ms[2],
            state_conv[l].reshape(Bs, (CONV_W - 1) * D_CONV),
            state_pool[l].reshape(Bs, (POOL_MAX - 1) * D_POOL),
            w_in_b[l], conv_w[l], pool_w_b[l], row(pool_scale), w_out_b[l],
            row(ln1_g), row(ln1_b))
        conv_s.append(cst.reshape(Bs, CONV_W - 1, D_CONV))
        pool_s.append(pst.reshape(Bs, POOL_MAX - 1, D_POOL))
        xs = _peer(xs.reshape(1, Bs, D), ms[3].reshape(1, Bs, D), ms[4].reshape(1, Bs, D),
                   ms[5].reshape(1, Bs, D), wqt_b[l], sk_b[l], cand_idx, u_b[l], vt_b[l],
                   row(ln2_g), row(ln2_b), tile=Bs).reshape(Bs, D)

    return (xp, xs.reshape(Bs, 1, D), jnp.stack(conv_p), jnp.stack(pool_p),
            jnp.stack(conv_s), jnp.stack(pool_s))
```

```python
import functools
import math

import jax
import jax.numpy as jnp
import numpy as np
from jax import lax
from jax.experimental import pallas as pl
from jax.experimental.pallas import tpu as pltpu

F32 = jnp.float32
BF16 = jnp.bfloat16

D_MODEL = 1024
DEPTH = 2
D_CONV = D_MODEL // 2
D_POOL = D_MODEL - D_CONV
CONV_W = 3
POOL_WINDOWS = (2, 4, 8, 16)
POOL_GROUP = D_POOL // len(POOL_WINDOWS)
POOL_MAX = max(POOL_WINDOWS)
D_IN = 3 * D_CONV + D_POOL
PEER_HEADS = 8
N_KEYS = 128
N_EXPERTS = N_KEYS * N_KEYS
PEER_TOPK = 16
D_KEY = 256
D_HALF = D_KEY // 2
PAST_LEN = 16384
ALPHA = (2 * DEPTH) ** 0.25
LN_EPS = 1e-5

LANES = 128
SUBLANES = 8
BF16_ROWS = 16
MXU_COLS = 256
VMEM_LIMIT_BYTES = 56 * 1024 * 1024

MIX_TILE = 1024
PEER_TILE = 512
EXPERT_CHUNK = 2048
DOT_ROWS = 512
KEYS_PER_CHUNK = EXPERT_CHUNK // N_KEYS
KEYS_PER_STEP = 1
ADA_COLS = 1536
TABLE_ROWS = 1024

_CAND_SEGMENTS = ((0, 16), (1, 8), (2, 8), (3, 8), (4, 4), (5, 4), (6, 4), (7, 4)) + tuple(
    (r, 1) for r in range(8, 16))
N_CAND = sum(n for _, n in _CAND_SEGMENTS)
assert N_CAND == 64


def _ln_stats(x):
    mu = jnp.mean(x, axis=-1, keepdims=True)
    xc = x - mu
    var = jnp.mean(xc * xc, axis=-1, keepdims=True)
    return xc * lax.rsqrt(var + LN_EPS)


def _dot(a, b):
    return jnp.dot(a, b, preferred_element_type=F32)


def _ada_kernel(c_ref, w_ref, b_ref, o_ref):
    c = c_ref[...]
    act = (c * jax.nn.sigmoid(c)).astype(BF16)
    o_ref[0] = _dot(act, w_ref[0].astype(BF16)) + b_ref[0]


def _ada_modulation(c_all, w_ada, b_ada):
    n = c_all.shape[0]
    n_cols = 6 * D_MODEL
    return pl.pallas_call(
        _ada_kernel,
        grid=(DEPTH, n_cols // ADA_COLS),
        in_specs=[
            pl.BlockSpec((n, D_MODEL), lambda l, j: (0, 0)),
            pl.BlockSpec((1, D_MODEL, ADA_COLS), lambda l, j: (l, 0, j)),
            pl.BlockSpec((1, 1, ADA_COLS), lambda l, j: (l, 0, j)),
        ],
        out_specs=pl.BlockSpec((1, n, ADA_COLS), lambda l, j: (l, 0, j)),
        out_shape=jax.ShapeDtypeStruct((DEPTH, n, n_cols), F32),
        compiler_params=pltpu.CompilerParams(
            dimension_semantics=("arbitrary", "arbitrary"),
            vmem_limit_bytes=VMEM_LIMIT_BYTES),
        name="ada_modulation",
    )(c_all, w_ada, b_ada.reshape(DEPTH, 1, n_cols))


def _mix_tail(x, g1, conv_out, pooled, poolw_ref, pscale, wout_ref, lng, lnb):
    mixed = [
        _dot(pooled[g].astype(BF16), poolw_ref[g]) for g in range(len(POOL_WINDOWS))
    ]
    pool_out = jnp.concatenate(mixed, axis=-1) * pscale
    cat = jnp.concatenate([conv_out, pool_out], axis=-1).astype(BF16)
    mix = _dot(cat, wout_ref[...])
    return _ln_stats(ALPHA * x + g1 * mix) * lng + lnb


def _mix_prompt_kernel(x_ref, sh_ref, sc_ref, g_ref, win_ref, cw_ref, poolw_ref,
                       pscale_ref, wout_ref, lng_ref, lnb_ref,
                       o_ref, cst_ref, pst_ref, ue_scr, pe_scr):
    j = pl.program_id(1)
    nt = pl.num_programs(1)
    T = MIX_TILE
    HU = SUBLANES
    HP = POOL_MAX

    @pl.when(j == 0)
    def _():
        ue_scr[0:HU, :] = jnp.zeros((HU, D_CONV), F32)
        pe_scr[0:HP, :] = jnp.zeros((HP, D_POOL), F32)

    x = x_ref[0]
    h = _ln_stats(x) * (1.0 + sc_ref[0]) + sh_ref[0]
    z = _dot(h.astype(BF16), win_ref[...])
    bg = z[:, 0:D_CONV]
    cg = z[:, D_CONV:2 * D_CONV]
    hc = z[:, 2 * D_CONV:3 * D_CONV]
    hp = z[:, 3 * D_CONV:]

    u = cg * hc
    ue_scr[HU:HU + T, :] = u
    pe_scr[HP:HP + T, :] = hp

    cw = cw_ref[...]
    y = (ue_scr[HU - 2:HU - 2 + T, :] * cw[0:1, :]
         + ue_scr[HU - 1:HU - 1 + T, :] * cw[1:2, :]
         + u * cw[2:3, :])
    conv_out = bg * y

    pos = (lax.broadcasted_iota(jnp.int32, (T, POOL_GROUP), 0) + j * T).astype(F32)
    pooled = []
    for g, w in enumerate(POOL_WINDOWS):
        lanes = slice(g * POOL_GROUP, (g + 1) * POOL_GROUP)
        win = hp[:, lanes]
        for k in range(1, w):
            win = win + pe_scr[HP - k:HP - k + T, lanes]
        cnt = jnp.minimum(pos + 1.0, float(w))
        pooled.append(win / cnt - hp[:, lanes])

    o_ref[0] = _mix_tail(x, g_ref[0], conv_out, pooled, poolw_ref, pscale_ref[...],
                         wout_ref, lng_ref[...], lnb_ref[...])

    @pl.when(j == nt - 1)
    def _():
        cst_ref[0] = ue_scr[HU + T - (CONV_W - 1):HU + T, :]
        pst_ref[0] = pe_scr[HP + T - (POOL_MAX - 1):HP + T, :]

    @pl.when(j < nt - 1)
    def _():
        ue_scr[0:HU, :] = ue_scr[T:T + HU, :]
        pe_scr[0:HP, :] = pe_scr[T:T + HP, :]


def _mix_prompt(x, sh, sc, g, win, cw, poolw, pscale, wout, lng, lnb):
    B, S, D = x.shape
    nt = S // MIX_TILE
    row = lambda b, j: (b, 0, 0)
    full2 = lambda b, j: (0, 0)
    full3 = lambda b, j: (0, 0, 0)
    return pl.pallas_call(
        _mix_prompt_kernel,
        grid=(B, nt),
        in_specs=[
            pl.BlockSpec((1, MIX_TILE, D), lambda b, j: (b, j, 0)),
            pl.BlockSpec((1, 1, D), row),
            pl.BlockSpec((1, 1, D), row),
            pl.BlockSpec((1, 1, D), row),
            pl.BlockSpec((D, D_IN), full2),
            pl.BlockSpec((CONV_W, D_CONV), full2),
            pl.BlockSpec((len(POOL_WINDOWS), POOL_GROUP, POOL_GROUP), full3),
            pl.BlockSpec((1, D_POOL), full2),
            pl.BlockSpec((D, D), full2),
            pl.BlockSpec((1, D), full2),
            pl.BlockSpec((1, D), full2),
        ],
        out_specs=[
            pl.BlockSpec((1, MIX_TILE, D), lambda b, j: (b, j, 0)),
            pl.BlockSpec((1, CONV_W - 1, D_CONV), row),
            pl.BlockSpec((1, POOL_MAX - 1, D_POOL), row),
        ],
        out_shape=[
            jax.ShapeDtypeStruct((B, S, D), F32),
            jax.ShapeDtypeStruct((B, CONV_W - 1, D_CONV), F32),
            jax.ShapeDtypeStruct((B, POOL_MAX - 1, D_POOL), F32),
        ],
        scratch_shapes=[
            pltpu.VMEM((MIX_TILE + SUBLANES, D_CONV), F32),
            pltpu.VMEM((MIX_TILE + POOL_MAX, D_POOL), F32),
        ],
        compiler_params=pltpu.CompilerParams(
            dimension_semantics=("arbitrary", "arbitrary"),
            vmem_limit_bytes=VMEM_LIMIT_BYTES),
        name="mix_prompt",
    )(x, sh, sc, g, win, cw, poolw, pscale, wout, lng, lnb)


def _mix_sample_kernel(x_ref, sh_ref, sc_ref, g_ref, cst_in_ref, pst_in_ref, win_ref,
                       cw_ref, poolw_ref, pscale_ref, wout_ref, lng_ref, lnb_ref,
                       o_ref, cst_ref, pst_ref):
    x = x_ref[...]
    h = _ln_stats(x) * (1.0 + sc_ref[...]) + sh_ref[...]
    z = _dot(h.astype(BF16), win_ref[...])
    bg = z[:, 0:D_CONV]
    cg = z[:, D_CONV:2 * D_CONV]
    hc = z[:, 2 * D_CONV:3 * D_CONV]
    hp = z[:, 3 * D_CONV:]

    u = cg * hc
    cw = cw_ref[...]
    prev2 = cst_in_ref[:, 0:D_CONV]
    prev1 = cst_in_ref[:, D_CONV:2 * D_CONV]
    conv_out = bg * (prev2 * cw[0:1, :] + prev1 * cw[1:2, :] + u * cw[2:3, :])
    cst_ref[:, 0:D_CONV] = prev1
    cst_ref[:, D_CONV:2 * D_CONV] = u

    n_hist = POOL_MAX - 1
    pooled = []
    for g, w in enumerate(POOL_WINDOWS):
        win = hp[:, g * POOL_GROUP:(g + 1) * POOL_GROUP]
        for k in range(1, w):
            lo = (n_hist - k) * D_POOL + g * POOL_GROUP
            win = win + pst_in_ref[:, lo:lo + POOL_GROUP]
        cnt = min(float(PAST_LEN) + 1.0, float(w))
        pooled.append(win / cnt - hp[:, g * POOL_GROUP:(g + 1) * POOL_GROUP])
    pst_ref[:, 0:(n_hist - 1) * D_POOL] = pst_in_ref[:, D_POOL:n_hist * D_POOL]
    pst_ref[:, (n_hist - 1) * D_POOL:n_hist * D_POOL] = hp

    o_ref[...] = _mix_tail(x, g_ref[...], conv_out, pooled, poolw_ref, pscale_ref[...],
                           wout_ref, lng_ref[...], lnb_ref[...])


def _mix_sample(x, sh, sc, g, cst, pst, win, cw, poolw, pscale, wout, lng, lnb):
    B, D = x.shape
    return pl.pallas_call(
        _mix_sample_kernel,
        out_shape=[
            jax.ShapeDtypeStruct((B, D), F32),
            jax.ShapeDtypeStruct(cst.shape, F32),
            jax.ShapeDtypeStruct(pst.shape, F32),
        ],
        compiler_params=pltpu.CompilerParams(vmem_limit_bytes=VMEM_LIMIT_BYTES),
        name="mix_sample",
    )(x, sh, sc, g, cst, pst, win, cw, poolw, pscale, wout, lng, lnb)


def _extract_top(s, idx, n_out):
    big = float(4 * N_EXPERTS)
    rank = jnp.full(s.shape, float(n_out), F32)
    vals = []
    for step in range(n_out):
        m = jnp.max(s, axis=0, keepdims=True)
        first = jnp.min(jnp.where(s == m, idx, big), axis=0, keepdims=True)
        sel = idx == first
        rank = jnp.where(sel, float(step), rank)
        s = jnp.where(sel, -jnp.inf, s)
        vals.append(m)
    return rank, vals


def _retrieve(s1, s2, cand_idx):
    key_idx = lax.broadcasted_iota(jnp.int32, s1.shape, 0).astype(F32)
    rank1, a = _extract_top(s1, key_idx, PEER_TOPK)
    rank2, b = _extract_top(s2, key_idx, PEER_TOPK)
    a_all = jnp.concatenate(a, axis=0)
    b_all = jnp.concatenate(b, axis=0)

    pieces = []
    for r, n in _CAND_SEGMENTS:
        pieces.append(a[r] + b_all[0:n, :])
    cand = jnp.concatenate(pieces, axis=0)
    crank, _ = _extract_top(cand, cand_idx, PEER_TOPK)
    chosen = crank < float(PEER_TOPK)

    cnt = []
    off = 0
    for r, n in _CAND_SEGMENTS:
        seg = jnp.where(chosen[off:off + n, :], 1.0, 0.0)
        cnt.append(jnp.sum(seg, axis=0, keepdims=True))
        off += n

    ea = jnp.exp(a_all - a[0])
    eb = jnp.exp(b_all - b[0])
    k_idx = lax.broadcasted_iota(jnp.int32, eb.shape, 0).astype(F32)
    z = jnp.zeros_like(a[0])
    for r in range(PEER_TOPK):
        prefix = jnp.sum(jnp.where(k_idx < cnt[r], eb, 0.0), axis=0, keepdims=True)
        z = z + ea[r:r + 1, :] * prefix
    inv_z = 1.0 / z

    cnt1 = jnp.zeros_like(s1)
    for r in range(PEER_TOPK):
        cnt1 = jnp.where(rank1 == float(r), cnt[r], cnt1)
    alpha = jnp.where(rank1 < float(PEER_TOPK), jnp.exp(s1 - a[0]) * inv_z, 0.0)
    beta = jnp.where(rank2 < float(PEER_TOPK), jnp.exp(s2 - b[0]), 0.0)
    return alpha, cnt1, beta, rank2


def _oddeven_network(n):
    pairs = []
    p = 1
    while p < n:
        k = p
        while k >= 1:
            for j in range(k % p, n - k, 2 * k):
                for i in range(min(k, n - j - k)):
                    if (i + j) // (2 * p) == (i + j + k) // (2 * p):
                        pairs.append((i + j, i + j + k))
            k //= 2
        p *= 2
    return pairs


_NET16 = _oddeven_network(PEER_TOPK)
_NET8 = _oddeven_network(SUBLANES)


def _sort_desc(v, net):
    v = list(v)
    for i, j in net:
        v[i], v[j] = jnp.maximum(v[i], v[j]), jnp.minimum(v[i], v[j])
    return v


def _bitonic_merge_desc(v):
    v = list(v)
    d = len(v) // 2
    while d >= 1:
        for i in range(len(v)):
            if (i & d) == 0:
                v[i], v[i + d] = jnp.maximum(v[i], v[i + d]), jnp.minimum(v[i], v[i + d])
        d //= 2
    return v


def _roll_sublanes(x, d):
    return pltpu.roll(x, d, 0)


def _top16_across_sublanes(v):
    for d in (4, 2, 1):
        p = [_roll_sublanes(x, d) for x in v]
        if len(v) == SUBLANES:
            v = _bitonic_merge_desc(v + p[::-1])
        else:
            v = _bitonic_merge_desc(
                [jnp.maximum(v[n], p[PEER_TOPK - 1 - n]) for n in range(PEER_TOPK)])
    return v


def _pack_rows(vals, sub):
    pick = sub & (len(vals) - 1)
    out = vals[0]
    for k in range(1, len(vals)):
        out = jnp.where(pick == k, vals[k], out)
    return out


def _sublane_sum(x):
    for d in (4, 2, 1):
        x = x + _roll_sublanes(x, d)
    return x


def _retrieve_by_threshold(s1, s2):
    n_v = N_KEYS // SUBLANES
    one = jnp.ones((SUBLANES, LANES), F32)
    zero = jnp.zeros((SUBLANES, LANES), F32)
    v1 = [s1[SUBLANES * n:SUBLANES * (n + 1), :] for n in range(n_v)]
    v2 = [s2[SUBLANES * n:SUBLANES * (n + 1), :] for n in range(n_v)]
    a = _top16_across_sublanes(_sort_desc(v1, _NET16))
    b = _top16_across_sublanes(_sort_desc(v2, _NET16))
    sub = lax.broadcasted_iota(jnp.int32, (SUBLANES, LANES), 0)

    b_lo = _pack_rows(b[0:8], sub)
    b_hi = _pack_rows(b[8:16], sub)
    b_4 = _pack_rows(b[0:4], sub)
    a_lo = _pack_rows(a[0:8], sub)
    a_hi = _pack_rows(a[8:16], sub)
    a45 = jnp.where(sub < 4, a[4], a[5])
    a67 = jnp.where(sub < 4, a[6], a[7])
    cand = [a[0] + b_lo, a[0] + b_hi, a[1] + b_lo, a[2] + b_lo, a[3] + b_lo,
            a45 + b_4, a67 + b_4, a_hi + b[0]]
    tau = _top16_across_sublanes(_sort_desc(cand, _NET8))[PEER_TOPK - 1]

    z_lo = zero
    z_hi = zero
    for k in range(PEER_TOPK):
        eb = jnp.exp(b[k] - b[0])
        z_lo = z_lo + jnp.where(a_lo + b[k] >= tau, eb, 0.0)
        z_hi = z_hi + jnp.where(a_hi + b[k] >= tau, eb, 0.0)
    z = _sublane_sum(jnp.exp(a_lo - a[0]) * z_lo + jnp.exp(a_hi - a[0]) * z_hi)
    inv_z = 1.0 / z

    alpha, cnt1, beta, rank2 = [], [], [], []
    n1 = zero
    n2 = zero
    n_pair = zero
    for n in range(n_v):
        x1 = v1[n]
        x2 = v2[n]
        in1 = x1 >= a[PEER_TOPK - 1]
        in2 = x2 >= b[PEER_TOPK - 1]
        c = zero
        r = zero
        for k in range(PEER_TOPK):
            c = jnp.where(x1 + b[k] >= tau, float(k + 1), c)
            r = jnp.where(b[k] > x2, float(k + 1), r)
        c = jnp.where(in1, c, 0.0)
        cnt1.append(c)
        rank2.append(r)
        alpha.append(jnp.where(in1, jnp.exp(x1 - a[0]) * inv_z, 0.0))
        beta.append(jnp.where(in2, jnp.exp(x2 - b[0]), 0.0))
        n1 = n1 + jnp.where(in1, one, zero)
        n2 = n2 + jnp.where(in2, one, zero)
        n_pair = n_pair + c
    want = float(PEER_TOPK)
    ok = ((_sublane_sum(n1) == want) & (_sublane_sum(n2) == want)
          & (_sublane_sum(n_pair) == want))
    cat = lambda xs: jnp.concatenate(xs, axis=0)
    return cat(alpha), cat(cnt1), cat(beta), cat(rank2), ok


def _gate_key_block(i, ib, n_tt, per_nb, act_scr, ct_scr, alpha_scr, cnt_scr, beta_scr, rho_scr):
    n_grp = N_KEYS // BF16_ROWS
    zero = jnp.zeros((), BF16)
    for tt in range(n_tt):
        nb, part = divmod(tt, per_nb)
        lanes = slice(part * LANES, (part + 1) * LANES)
        gate = [[None] * n_grp for _ in range(KEYS_PER_STEP)]
        for h in range(PEER_HEADS):
            al = [jnp.broadcast_to(alpha_scr[h, tt, pl.ds(i + d, 1), :],
                                   (BF16_ROWS, LANES)).astype(BF16) for d in range(KEYS_PER_STEP)]
            cc = [jnp.broadcast_to(cnt_scr[h, tt, pl.ds(i + d, 1), :],
                                   (BF16_ROWS, LANES)).astype(BF16) for d in range(KEYS_PER_STEP)]
            for g in range(n_grp):
                grp = slice(g * BF16_ROWS, (g + 1) * BF16_ROWS)
                rho = rho_scr[h, tt, grp, :]
                beta = beta_scr[h, tt, grp, :]
                for d in range(KEYS_PER_STEP):
                    term = al[d] * jnp.where(rho < cc[d], beta, zero)
                    gate[d][g] = term if gate[d][g] is None else gate[d][g] + term
        for d in range(KEYS_PER_STEP):
            for g in range(n_grp):
                lo = (ib + d) * N_KEYS + g * BF16_ROWS
                rows = slice(lo, lo + BF16_ROWS)
                ct_scr[nb, rows, lanes] = act_scr[nb, rows, lanes] * gate[d][g]


def _peer_kernel(x_ref, sh_ref, sc_ref, g_ref, wqt_ref, sk_ref, cidx_ref, u_ref, vt_ref,
                 lng_ref, lnb_ref, o_ref,
                 h2s_scr, h2g_scr, ft_scr, s_scr, alpha_scr, cnt_scr, beta_scr, rho_scr,
                 act_scr, ct_scr, *, tile, width):
    e = pl.program_id(2)
    n_chunks = pl.num_programs(2) - 1
    n_tt = tile // LANES
    n_nb = tile // width
    per_nb = width // LANES

    def activations():
        for nb in range(n_nb):
            for r0 in range(0, EXPERT_CHUNK, DOT_ROWS):
                t = _dot(u_ref[r0:r0 + DOT_ROWS, :], h2g_scr[nb])
                act_scr[nb, r0:r0 + DOT_ROWS, :] = (t + t * lax.erf(t)).astype(BF16)

    def accumulate():
        for nb in range(n_nb):
            for r0 in range(0, D_MODEL, DOT_ROWS):
                ft_scr[nb, r0:r0 + DOT_ROWS, :] += _dot(vt_ref[r0:r0 + DOT_ROWS, :], ct_scr[nb])

    @pl.when(e == 0)
    def _first():
        x = x_ref[0]
        h2 = _ln_stats(x) * (1.0 + sc_ref[0]) + sh_ref[0]
        h2t = h2.T
        h2s_scr[...] = h2t.astype(BF16)
        for nb in range(n_nb):
            blk = h2t[:, nb * width:(nb + 1) * width]
            h2g_scr[nb] = (blk * math.sqrt(0.5)).astype(BF16)
        ft_scr[...] = jnp.zeros(ft_scr.shape, F32)
        for h0 in range(0, PEER_HEADS, 2):
            qt = _dot(wqt_ref[h0 * D_KEY:(h0 + 2) * D_KEY, :], h2s_scr[...])
            for hp in range(4):
                h, p = h0 + hp // 2, hp % 2
                s = _dot(sk_ref[h, p], qt[hp * D_HALF:(hp + 1) * D_HALF, :].astype(BF16))
                for tt in range(n_tt):
                    s_scr[p, h, tt] = s[:, tt * LANES:(tt + 1) * LANES]

        cand_idx = cidx_ref[...]

        def put(h, tt, alpha, cnt1, beta, rank2):
            alpha_scr[h, tt] = alpha * math.sqrt(0.5)
            cnt_scr[h, tt] = cnt1
            beta_scr[h, tt] = beta.astype(BF16)
            rho_scr[h, tt] = rank2.astype(BF16)

        def topk_body(it, all_ok):
            h = it // n_tt
            tt = it % n_tt
            alpha, cnt1, beta, rank2, ok = _retrieve_by_threshold(s_scr[0, h, tt], s_scr[1, h, tt])
            put(h, tt, alpha, cnt1, beta, rank2)
            return jnp.minimum(all_ok, jnp.where(ok, 1.0, 0.0))

        all_ok = lax.fori_loop(0, PEER_HEADS * n_tt, topk_body,
                               jnp.ones((SUBLANES, LANES), F32))

        @pl.when(jnp.min(all_ok) < 0.5)
        def _():
            def exact_body(it, carry):
                h = it // n_tt
                tt = it % n_tt
                put(h, tt, *_retrieve(s_scr[0, h, tt], s_scr[1, h, tt], cand_idx))
                return carry

            lax.fori_loop(0, PEER_HEADS * n_tt, exact_body, 0)

        activations()

    @pl.when(jnp.logical_and(e > 0, e < n_chunks))
    def _mxu():
        accumulate()
        activations()

    @pl.when(e < n_chunks)
    def _gate():
        for ib in range(0, KEYS_PER_CHUNK, KEYS_PER_STEP):
            _gate_key_block(e * KEYS_PER_CHUNK + ib, ib, n_tt, per_nb, act_scr, ct_scr,
                            alpha_scr, cnt_scr, beta_scr, rho_scr)

    @pl.when(e == n_chunks)
    def _last():
        accumulate()
        for nb in range(n_nb):
            f = ft_scr[nb].T
            rows = slice(nb * width, (nb + 1) * width)
            g = g_ref[0] if g_ref.shape[1] == 1 else g_ref[0, rows, :]
            y = ALPHA * x_ref[0, rows, :] + g * f
            o_ref[0, rows, :] = _ln_stats(y) * lng_ref[...] + lnb_ref[...]


def _peer(x, sh, sc, g, wqt, sk, cand_idx, u, vt, lng, lnb, *, tile):
    B, S, D = x.shape
    nt = S // tile
    ne = N_EXPERTS // EXPERT_CHUNK
    n_tt = tile // LANES
    width = min(tile, MXU_COLS)
    n_nb = tile // width
    R = sh.shape[1]
    if R == 1:
        mod_spec = pl.BlockSpec((1, 1, D), lambda b, t, e: (b, 0, 0))
    else:
        mod_spec = pl.BlockSpec((1, tile, D), lambda b, t, e: (b, t, 0))
    kern = functools.partial(_peer_kernel, tile=tile, width=width)
    return pl.pallas_call(
        kern,
        grid=(B, nt, ne + 1),
        in_specs=[
            pl.BlockSpec((1, tile, D), lambda b, t, e: (b, t, 0)),
            mod_spec, mod_spec, mod_spec,
            pl.BlockSpec((PEER_HEADS * D_KEY, D), lambda b, t, e: (0, 0)),
            pl.BlockSpec((PEER_HEADS, 2, N_KEYS, D_HALF), lambda b, t, e: (0, 0, 0, 0)),
            pl.BlockSpec((N_CAND, LANES), lambda b, t, e: (0, 0)),
            pl.BlockSpec((EXPERT_CHUNK, D), lambda b, t, e: (jnp.minimum(e, ne - 1), 0)),
            pl.BlockSpec((D, EXPERT_CHUNK), lambda b, t, e: (0, jnp.clip(e - 1, 0, ne - 1))),
            pl.BlockSpec((1, D), lambda b, t, e: (0, 0)),
            pl.BlockSpec((1, D), lambda b, t, e: (0, 0)),
        ],
        out_specs=pl.BlockSpec((1, tile, D), lambda b, t, e: (b, t, 0)),
        out_shape=jax.ShapeDtypeStruct((B, S, D), F32),
        scratch_shapes=[
            pltpu.VMEM((D, tile), BF16),
            pltpu.VMEM((n_nb, D, width), BF16),
            pltpu.VMEM((n_nb, D, width), F32),
            pltpu.VMEM((2, PEER_HEADS, n_tt, N_KEYS, LANES), F32),
            pltpu.VMEM((PEER_HEADS, n_tt, N_KEYS, LANES), F32),
            pltpu.VMEM((PEER_HEADS, n_tt, N_KEYS, LANES), F32),
            pltpu.VMEM((PEER_HEADS, n_tt, N_KEYS, LANES), BF16),
            pltpu.VMEM((PEER_HEADS, n_tt, N_KEYS, LANES), BF16),
            pltpu.VMEM((n_nb, EXPERT_CHUNK, width), BF16),
            pltpu.VMEM((n_nb, EXPERT_CHUNK, width), BF16),
        ],
        compiler_params=pltpu.CompilerParams(
            dimension_semantics=("arbitrary", "arbitrary", "arbitrary"),
            vmem_limit_bytes=VMEM_LIMIT_BYTES),
        name="peer",
    )(x, sh, sc, g, wqt, sk, cand_idx, u, vt, lng, lnb)


def _expert_tables_kernel(u_ref, v_ref, ub_ref, vt_ref):
    ub_ref[...] = u_ref[0].astype(BF16)
    vt_ref[...] = v_ref[0].T.astype(BF16)


def _expert_tables(u_experts, v_experts, layer):
    _, E, D = u_experts.shape
    in_spec = pl.BlockSpec((1, TABLE_ROWS, D), lambda i: (layer, i, 0))
    return pl.pallas_call(
        _expert_tables_kernel,
        grid=(E // TABLE_ROWS,),
        in_specs=[in_spec, in_spec],
        out_specs=[pl.BlockSpec((TABLE_ROWS, D), lambda i: (i, 0)),
                   pl.BlockSpec((D, TABLE_ROWS), lambda i: (0, i))],
        out_shape=[jax.ShapeDtypeStruct((E, D), BF16), jax.ShapeDtypeStruct((D, E), BF16)],
        compiler_params=pltpu.CompilerParams(
            dimension_semantics=("arbitrary",), vmem_limit_bytes=VMEM_LIMIT_BYTES),
        name="expert_tables",
    )(u_experts, v_experts)


def _cand_index_table():
    idx = []
    for r, n in _CAND_SEGMENTS:
        idx.extend(r * PEER_TOPK + k for k in range(n))
    col = np.asarray(idx, np.float32)[:, None]
    return jnp.asarray(np.broadcast_to(col, (N_CAND, LANES)).copy())


def kernel(x_prompt, x_sample, state_conv, state_pool, c_prompt, c_sample, w_ada, b_ada, w_in, conv_w, pool_w, pool_scale, w_out, ln1_g, ln1_b, w_q, sub_keys, u_experts, v_experts, ln2_g, ln2_b):
    B, S, D = x_prompt.shape
    Bs = x_sample.shape[0]

    cand_idx = _cand_index_table()

    c_all = jnp.concatenate([c_prompt, c_sample], axis=0)
    mod = _ada_modulation(c_all, w_ada, b_ada)

    xp = x_prompt
    xs = x_sample.reshape(Bs, D)
    conv_p, pool_p, conv_s, pool_s = [], [], [], []
    for l in range(DEPTH):
        mp = [mod[l, :B, k * D:(k + 1) * D].reshape(B, 1, D) for k in range(6)]
        ms = [mod[l, B:, k * D:(k + 1) * D] for k in range(6)]
        row = lambda a: a[l].reshape(1, -1)
        w_in_l = w_in[l].astype(BF16)
        w_out_l = w_out[l].astype(BF16)
        pool_w_l = pool_w[l].astype(BF16)
        wqt_l = w_q[l].T.astype(BF16)
        sk_l = sub_keys[l].astype(BF16)
        u_l, vt_l = _expert_tables(u_experts, v_experts, l)

        xp, cst, pst = _mix_prompt(xp, mp[0], mp[1], mp[2], w_in_l, conv_w[l], pool_w_l,
                                   row(pool_scale), w_out_l, row(ln1_g), row(ln1_b))
        conv_p.append(cst)
        pool_p.append(pst)
        xp = _peer(xp, mp[3], mp[4], mp[5], wqt_l, sk_l, cand_idx, u_l, vt_l,
                   row(ln2_g), row(ln2_b), tile=PEER_TILE)

        xs, cst, pst = _mix_sample(
            xs, ms[0], ms[1], ms[2],
            state_conv[l].reshape(Bs, (CONV_W - 1) * D_CONV),
            state_pool[l].reshape(Bs, (POOL_MAX - 1) * D_POOL),
            w_in_l, conv_w[l], pool_w_l, row(pool_scale), w_out_l,
            row(ln1_g), row(ln1_b))
        conv_s.append(cst.reshape(Bs, CONV_W - 1, D_CONV))
        pool_s.append(pst.reshape(Bs, POOL_MAX - 1, D_POOL))
        xs = _peer(xs.reshape(1, Bs, D), ms[3].reshape(1, Bs, D), ms[4].reshape(1, Bs, D),
                   ms[5].reshape(1, Bs, D), wqt_l, sk_l, cand_idx, u_l, vt_l,
                   row(ln2_g), row(ln2_b), tile=Bs).reshape(Bs, D)

    return (xp, xs.reshape(Bs, 1, D), jnp.stack(conv_p), jnp.stack(pool_p),
            jnp.stack(conv_s), jnp.stack(pool_s))
```

```python
import functools
import math

import jax
import jax.numpy as jnp
import numpy as np
from jax import lax
from jax.experimental import pallas as pl
from jax.experimental.pallas import tpu as pltpu

F32 = jnp.float32
BF16 = jnp.bfloat16

D_MODEL = 1024
DEPTH = 2
D_CONV = D_MODEL // 2
D_POOL = D_MODEL - D_CONV
CONV_W = 3
POOL_WINDOWS = (2, 4, 8, 16)
POOL_GROUP = D_POOL // len(POOL_WINDOWS)
POOL_MAX = max(POOL_WINDOWS)
D_IN = 3 * D_CONV + D_POOL
PEER_HEADS = 8
N_KEYS = 128
N_EXPERTS = N_KEYS * N_KEYS
PEER_TOPK = 16
D_KEY = 256
D_HALF = D_KEY // 2
PAST_LEN = 16384
ALPHA = (2 * DEPTH) ** 0.25
LN_EPS = 1e-5

LANES = 128
SUBLANES = 8
BF16_ROWS = 16
MXU_COLS = 256
VMEM_LIMIT_BYTES = 56 * 1024 * 1024

MIX_TILE = 1024
PEER_TILE = 512
EXPERT_CHUNK = 2048
DOT_ROWS = 512
KEYS_PER_CHUNK = EXPERT_CHUNK // N_KEYS
KEYS_PER_STEP = 1
ADA_COLS = 1536
TABLE_ROWS = 1024

_CAND_SEGMENTS = ((0, 16), (1, 8), (2, 8), (3, 8), (4, 4), (5, 4), (6, 4), (7, 4)) + tuple(
    (r, 1) for r in range(8, 16))
N_CAND = sum(n for _, n in _CAND_SEGMENTS)
assert N_CAND == 64


def _ln_stats(x):
    mu = jnp.mean(x, axis=-1, keepdims=True)
    xc = x - mu
    var = jnp.mean(xc * xc, axis=-1, keepdims=True)
    return xc * lax.rsqrt(var + LN_EPS)


def _dot(a, b):
    return jnp.dot(a, b, preferred_element_type=F32)


def _ada_kernel(c_ref, w_ref, b_ref, o_ref):
    c = c_ref[...]
    act = (c * jax.nn.sigmoid(c)).astype(BF16)
    o_ref[0] = _dot(act, w_ref[0].astype(BF16)) + b_ref[0]


def _ada_modulation(c_all, w_ada, b_ada):
    n = c_all.shape[0]
    n_cols = 6 * D_MODEL
    return pl.pallas_call(
        _ada_kernel,
        grid=(DEPTH, n_cols // ADA_COLS),
        in_specs=[
            pl.BlockSpec((n, D_MODEL), lambda l, j: (0, 0)),
            pl.BlockSpec((1, D_MODEL, ADA_COLS), lambda l, j: (l, 0, j)),
            pl.BlockSpec((1, 1, ADA_COLS), lambda l, j: (l, 0, j)),
        ],
        out_specs=pl.BlockSpec((1, n, ADA_COLS), lambda l, j: (l, 0, j)),
        out_shape=jax.ShapeDtypeStruct((DEPTH, n, n_cols), F32),
        compiler_params=pltpu.CompilerParams(
            dimension_semantics=("arbitrary", "arbitrary"),
            vmem_limit_bytes=VMEM_LIMIT_BYTES),
        name="ada_modulation",
    )(c_all, w_ada, b_ada.reshape(DEPTH, 1, n_cols))


def _mix_tail(x, g1, conv_out, pooled, poolw_ref, pscale, wout_ref, lng, lnb):
    mixed = [
        _dot(pooled[g].astype(BF16), poolw_ref[g]) for g in range(len(POOL_WINDOWS))
    ]
    pool_out = jnp.concatenate(mixed, axis=-1) * pscale
    mix = (_dot(conv_out.astype(BF16), wout_ref[0:D_CONV, :])
           + _dot(pool_out.astype(BF16), wout_ref[D_CONV:, :]))
    return _ln_stats(ALPHA * x + g1 * mix) * lng + lnb


def _mix_prompt_kernel(x_ref, sh_ref, sc_ref, g_ref, win_ref, cw_ref, poolw_ref,
                       pscale_ref, wout_ref, lng_ref, lnb_ref,
                       o_ref, cst_ref, pst_ref, ue_scr, pe_scr):
    j = pl.program_id(1)
    nt = pl.num_programs(1)
    T = MIX_TILE
    HU = SUBLANES
    HP = POOL_MAX

    @pl.when(j == 0)
    def _():
        ue_scr[0:HU, :] = jnp.zeros((HU, D_CONV), F32)
        pe_scr[0:HP, :] = jnp.zeros((HP, D_POOL), F32)

    x = x_ref[0]
    h = (_ln_stats(x) * (1.0 + sc_ref[0]) + sh_ref[0]).astype(BF16)
    u = (_dot(h, win_ref[:, D_CONV:2 * D_CONV]) * _dot(h, win_ref[:, 2 * D_CONV:3 * D_CONV]))
    ue_scr[HU:HU + T, :] = u
    hp = _dot(h, win_ref[:, 3 * D_CONV:])
    pe_scr[HP:HP + T, :] = hp
    bg = _dot(h, win_ref[:, 0:D_CONV])

    cw = cw_ref[...]
    y = (ue_scr[HU - 2:HU - 2 + T, :] * cw[0:1, :]
         + ue_scr[HU - 1:HU - 1 + T, :] * cw[1:2, :]
         + u * cw[2:3, :])
    conv_out = bg * y

    pos = (lax.broadcasted_iota(jnp.int32, (T, POOL_GROUP), 0) + j * T).astype(F32)
    pooled = []
    for g, w in enumerate(POOL_WINDOWS):
        lanes = slice(g * POOL_GROUP, (g + 1) * POOL_GROUP)
        win = hp[:, lanes]
        for k in range(1, w):
            win = win + pe_scr[HP - k:HP - k + T, lanes]
        cnt = jnp.minimum(pos + 1.0, float(w))
        pooled.append(win / cnt - hp[:, lanes])

    o_ref[0] = _mix_tail(x, g_ref[0], conv_out, pooled, poolw_ref, pscale_ref[...],
                         wout_ref, lng_ref[...], lnb_ref[...])

    @pl.when(j == nt - 1)
    def _():
        cst_ref[0] = ue_scr[HU + T - (CONV_W - 1):HU + T, :]
        pst_ref[0] = pe_scr[HP + T - (POOL_MAX - 1):HP + T, :]

    @pl.when(j < nt - 1)
    def _():
        ue_scr[0:HU, :] = ue_scr[T:T + HU, :]
        pe_scr[0:HP, :] = pe_scr[T:T + HP, :]


def _mix_prompt(x, sh, sc, g, win, cw, poolw, pscale, wout, lng, lnb):
    B, S, D = x.shape
    nt = S // MIX_TILE
    row = lambda b, j: (b, 0, 0)
    full2 = lambda b, j: (0, 0)
    full3 = lambda b, j: (0, 0, 0)
    return pl.pallas_call(
        _mix_prompt_kernel,
        grid=(B, nt),
        in_specs=[
            pl.BlockSpec((1, MIX_TILE, D), lambda b, j: (b, j, 0)),
            pl.BlockSpec((1, 1, D), row),
            pl.BlockSpec((1, 1, D), row),
            pl.BlockSpec((1, 1, D), row),
            pl.BlockSpec((D, D_IN), full2),
            pl.BlockSpec((CONV_W, D_CONV), full2),
            pl.BlockSpec((len(POOL_WINDOWS), POOL_GROUP, POOL_GROUP), full3),
            pl.BlockSpec((1, D_POOL), full2),
            pl.BlockSpec((D, D), full2),
            pl.BlockSpec((1, D), full2),
            pl.BlockSpec((1, D), full2),
        ],
        out_specs=[
            pl.BlockSpec((1, MIX_TILE, D), lambda b, j: (b, j, 0)),
            pl.BlockSpec((1, CONV_W - 1, D_CONV), row),
            pl.BlockSpec((1, POOL_MAX - 1, D_POOL), row),
        ],
        out_shape=[
            jax.ShapeDtypeStruct((B, S, D), F32),
            jax.ShapeDtypeStruct((B, CONV_W - 1, D_CONV), F32),
            jax.ShapeDtypeStruct((B, POOL_MAX - 1, D_POOL), F32),
        ],
        scratch_shapes=[
            pltpu.VMEM((MIX_TILE + SUBLANES, D_CONV), F32),
            pltpu.VMEM((MIX_TILE + POOL_MAX, D_POOL), F32),
        ],
        compiler_params=pltpu.CompilerParams(
            dimension_semantics=("arbitrary", "arbitrary"),
            vmem_limit_bytes=VMEM_LIMIT_BYTES),
        name="mix_prompt",
    )(x, sh, sc, g, win, cw, poolw, pscale, wout, lng, lnb)


def _mix_sample_kernel(x_ref, sh_ref, sc_ref, g_ref, cst_in_ref, pst_in_ref, win_ref,
                       cw_ref, poolw_ref, pscale_ref, wout_ref, lng_ref, lnb_ref,
                       o_ref, cst_ref, pst_ref):
    x = x_ref[...]
    h = (_ln_stats(x) * (1.0 + sc_ref[...]) + sh_ref[...]).astype(BF16)
    u = (_dot(h, win_ref[:, D_CONV:2 * D_CONV]) * _dot(h, win_ref[:, 2 * D_CONV:3 * D_CONV]))
    hp = _dot(h, win_ref[:, 3 * D_CONV:])
    bg = _dot(h, win_ref[:, 0:D_CONV])
    cw = cw_ref[...]
    prev2 = cst_in_ref[:, 0:D_CONV]
    prev1 = cst_in_ref[:, D_CONV:2 * D_CONV]
    conv_out = bg * (prev2 * cw[0:1, :] + prev1 * cw[1:2, :] + u * cw[2:3, :])
    cst_ref[:, 0:D_CONV] = prev1
    cst_ref[:, D_CONV:2 * D_CONV] = u

    n_hist = POOL_MAX - 1
    pooled = []
    for g, w in enumerate(POOL_WINDOWS):
        win = hp[:, g * POOL_GROUP:(g + 1) * POOL_GROUP]
        for k in range(1, w):
            lo = (n_hist - k) * D_POOL + g * POOL_GROUP
            win = win + pst_in_ref[:, lo:lo + POOL_GROUP]
        cnt = min(float(PAST_LEN) + 1.0, float(w))
        pooled.append(win / cnt - hp[:, g * POOL_GROUP:(g + 1) * POOL_GROUP])
    pst_ref[:, 0:(n_hist - 1) * D_POOL] = pst_in_ref[:, D_POOL:n_hist * D_POOL]
    pst_ref[:, (n_hist - 1) * D_POOL:n_hist * D_POOL] = hp

    o_ref[...] = _mix_tail(x, g_ref[...], conv_out, pooled, poolw_ref, pscale_ref[...],
                           wout_ref, lng_ref[...], lnb_ref[...])


def _mix_sample(x, sh, sc, g, cst, pst, win, cw, poolw, pscale, wout, lng, lnb):
    B, D = x.shape
    return pl.pallas_call(
        _mix_sample_kernel,
        out_shape=[
            jax.ShapeDtypeStruct((B, D), F32),
            jax.ShapeDtypeStruct(cst.shape, F32),
            jax.ShapeDtypeStruct(pst.shape, F32),
        ],
        compiler_params=pltpu.CompilerParams(vmem_limit_bytes=VMEM_LIMIT_BYTES),
        name="mix_sample",
    )(x, sh, sc, g, cst, pst, win, cw, poolw, pscale, wout, lng, lnb)


def _extract_top(s, idx, n_out):
    big = float(4 * N_EXPERTS)
    rank = jnp.full(s.shape, float(n_out), F32)
    vals = []
    for step in range(n_out):
        m = jnp.max(s, axis=0, keepdims=True)
        first = jnp.min(jnp.where(s == m, idx, big), axis=0, keepdims=True)
        sel = idx == first
        rank = jnp.where(sel, float(step), rank)
        s = jnp.where(sel, -jnp.inf, s)
        vals.append(m)
    return rank, vals


def _retrieve(s1, s2, cand_idx):
    key_idx = lax.broadcasted_iota(jnp.int32, s1.shape, 0).astype(F32)
    rank1, a = _extract_top(s1, key_idx, PEER_TOPK)
    rank2, b = _extract_top(s2, key_idx, PEER_TOPK)
    a_all = jnp.concatenate(a, axis=0)
    b_all = jnp.concatenate(b, axis=0)

    pieces = []
    for r, n in _CAND_SEGMENTS:
        pieces.append(a[r] + b_all[0:n, :])
    cand = jnp.concatenate(pieces, axis=0)
    crank, _ = _extract_top(cand, cand_idx, PEER_TOPK)
    chosen = crank < float(PEER_TOPK)

    cnt = []
    off = 0
    for r, n in _CAND_SEGMENTS:
        seg = jnp.where(chosen[off:off + n, :], 1.0, 0.0)
        cnt.append(jnp.sum(seg, axis=0, keepdims=True))
        off += n

    ea = jnp.exp(a_all - a[0])
    eb = jnp.exp(b_all - b[0])
    k_idx = lax.broadcasted_iota(jnp.int32, eb.shape, 0).astype(F32)
    z = jnp.zeros_like(a[0])
    for r in range(PEER_TOPK):
        prefix = jnp.sum(jnp.where(k_idx < cnt[r], eb, 0.0), axis=0, keepdims=True)
        z = z + ea[r:r + 1, :] * prefix
    inv_z = 1.0 / z

    cnt1 = jnp.zeros_like(s1)
    for r in range(PEER_TOPK):
        cnt1 = jnp.where(rank1 == float(r), cnt[r], cnt1)
    alpha = jnp.where(rank1 < float(PEER_TOPK), jnp.exp(s1 - a[0]) * inv_z, 0.0)
    beta = jnp.where(rank2 < float(PEER_TOPK), jnp.exp(s2 - b[0]), 0.0)
    return alpha, cnt1, beta, rank2


def _oddeven_network(n):
    pairs = []
    p = 1
    while p < n:
        k = p
        while k >= 1:
            for j in range(k % p, n - k, 2 * k):
                for i in range(min(k, n - j - k)):
                    if (i + j) // (2 * p) == (i + j + k) // (2 * p):
                        pairs.append((i + j, i + j + k))
            k //= 2
        p *= 2
    return pairs


_NET16 = _oddeven_network(PEER_TOPK)
_NET8 = _oddeven_network(SUBLANES)


def _sort_desc(v, net):
    v = list(v)
    for i, j in net:
        v[i], v[j] = jnp.maximum(v[i], v[j]), jnp.minimum(v[i], v[j])
    return v


def _bitonic_merge_desc(v):
    v = list(v)
    d = len(v) // 2
    while d >= 1:
        for i in range(len(v)):
            if (i & d) == 0:
                v[i], v[i + d] = jnp.maximum(v[i], v[i + d]), jnp.minimum(v[i], v[i + d])
        d //= 2
    return v


def _roll_sublanes(x, d):
    return pltpu.roll(x, d, 0)


def _top16_across_sublanes(v):
    for d in (4, 2, 1):
        p = [_roll_sublanes(x, d) for x in v]
        if len(v) == SUBLANES:
            v = _bitonic_merge_desc(v + p[::-1])
        else:
            v = _bitonic_merge_desc(
                [jnp.maximum(v[n], p[PEER_TOPK - 1 - n]) for n in range(PEER_TOPK)])
    return v


def _pack_rows(vals, sub):
    pick = sub & (len(vals) - 1)
    out = vals[0]
    for k in range(1, len(vals)):
        out = jnp.where(pick == k, vals[k], out)
    return out


def _sublane_sum(x):
    for d in (4, 2, 1):
        x = x + _roll_sublanes(x, d)
    return x


def _retrieve_by_threshold(s1, s2):
    n_v = N_KEYS // SUBLANES
    one = jnp.ones((SUBLANES, LANES), F32)
    zero = jnp.zeros((SUBLANES, LANES), F32)
    v1 = [s1[SUBLANES * n:SUBLANES * (n + 1), :] for n in range(n_v)]
    v2 = [s2[SUBLANES * n:SUBLANES * (n + 1), :] for n in range(n_v)]
    a = _top16_across_sublanes(_sort_desc(v1, _NET16))
    b = _top16_across_sublanes(_sort_desc(v2, _NET16))
    sub = lax.broadcasted_iota(jnp.int32, (SUBLANES, LANES), 0)

    b_lo = _pack_rows(b[0:8], sub)
    b_hi = _pack_rows(b[8:16], sub)
    b_4 = _pack_rows(b[0:4], sub)
    a_lo = _pack_rows(a[0:8], sub)
    a_hi = _pack_rows(a[8:16], sub)
    a45 = jnp.where(sub < 4, a[4], a[5])
    a67 = jnp.where(sub < 4, a[6], a[7])
    cand = [a[0] + b_lo, a[0] + b_hi, a[1] + b_lo, a[2] + b_lo, a[3] + b_lo,
            a45 + b_4, a67 + b_4, a_hi + b[0]]
    tau = _top16_across_sublanes(_sort_desc(cand, _NET8))[PEER_TOPK - 1]

    z_lo = zero
    z_hi = zero
    for k in range(PEER_TOPK):
        eb = jnp.exp(b[k] - b[0])
        z_lo = z_lo + jnp.where(a_lo + b[k] >= tau, eb, 0.0)
        z_hi = z_hi + jnp.where(a_hi + b[k] >= tau, eb, 0.0)
    z = _sublane_sum(jnp.exp(a_lo - a[0]) * z_lo + jnp.exp(a_hi - a[0]) * z_hi)
    inv_z = 1.0 / z

    alpha, cnt1, beta, rank2 = [], [], [], []
    n1 = zero
    n2 = zero
    n_pair = zero
    for n in range(n_v):
        x1 = v1[n]
        x2 = v2[n]
        in1 = x1 >= a[PEER_TOPK - 1]
        in2 = x2 >= b[PEER_TOPK - 1]
        c = zero
        r = zero
        for k in range(PEER_TOPK):
            c = jnp.where(x1 + b[k] >= tau, float(k + 1), c)
            r = jnp.where(b[k] > x2, float(k + 1), r)
        c = jnp.where(in1, c, 0.0)
        cnt1.append(c)
        rank2.append(r)
        alpha.append(jnp.where(in1, jnp.exp(x1 - a[0]) * inv_z, 0.0))
        beta.append(jnp.where(in2, jnp.exp(x2 - b[0]), 0.0))
        n1 = n1 + jnp.where(in1, one, zero)
        n2 = n2 + jnp.where(in2, one, zero)
        n_pair = n_pair + c
    want = float(PEER_TOPK)
    ok = ((_sublane_sum(n1) == want) & (_sublane_sum(n2) == want)
          & (_sublane_sum(n_pair) == want))
    cat = lambda xs: jnp.concatenate(xs, axis=0)
    return cat(alpha), cat(cnt1), cat(beta), cat(rank2), ok


def _gate_key_block(i, ib, n_tt, per_nb, act_scr, ct_scr, alpha_scr, cnt_scr, beta_scr, rho_scr):
    n_grp = N_KEYS // BF16_ROWS
    zero = jnp.zeros((), BF16)
    for tt in range(n_tt):
        nb, part = divmod(tt, per_nb)
        lanes = slice(part * LANES, (part + 1) * LANES)
        gate = [[None] * n_grp for _ in range(KEYS_PER_STEP)]
        for h in range(PEER_HEADS):
            al = [jnp.broadcast_to(alpha_scr[h, tt, pl.ds(i + d, 1), :],
                                   (BF16_ROWS, LANES)).astype(BF16) for d in range(KEYS_PER_STEP)]
            cc = [jnp.broadcast_to(cnt_scr[h, tt, pl.ds(i + d, 1), :],
                                   (BF16_ROWS, LANES)).astype(BF16) for d in range(KEYS_PER_STEP)]
            for g in range(n_grp):
                grp = slice(g * BF16_ROWS, (g + 1) * BF16_ROWS)
                rho = rho_scr[h, tt, grp, :]
                beta = beta_scr[h, tt, grp, :]
                for d in range(KEYS_PER_STEP):
                    term = al[d] * jnp.where(rho < cc[d], beta, zero)
                    gate[d][g] = term if gate[d][g] is None else gate[d][g] + term
        for d in range(KEYS_PER_STEP):
            for g in range(n_grp):
                lo = (ib + d) * N_KEYS + g * BF16_ROWS
                rows = slice(lo, lo + BF16_ROWS)
                ct_scr[nb, rows, lanes] = act_scr[nb, rows, lanes] * gate[d][g]


def _peer_kernel(x_ref, sh_ref, sc_ref, g_ref, wqt_ref, sk_ref, cidx_ref, u_ref, vt_ref,
                 lng_ref, lnb_ref, o_ref,
                 h2s_scr, h2g_scr, ft_scr, s_scr, alpha_scr, cnt_scr, beta_scr, rho_scr,
                 act_scr, ct_scr, *, tile, width):
    e = pl.program_id(2)
    n_chunks = pl.num_programs(2) - 1
    n_tt = tile // LANES
    n_nb = tile // width
    per_nb = width // LANES

    def activations():
        for nb in range(n_nb):
            for r0 in range(0, EXPERT_CHUNK, DOT_ROWS):
                t = _dot(u_ref[r0:r0 + DOT_ROWS, :], h2g_scr[nb])
                act_scr[nb, r0:r0 + DOT_ROWS, :] = (t + t * lax.erf(t)).astype(BF16)

    def accumulate():
        for nb in range(n_nb):
            for r0 in range(0, D_MODEL, DOT_ROWS):
                ft_scr[nb, r0:r0 + DOT_ROWS, :] += _dot(vt_ref[r0:r0 + DOT_ROWS, :], ct_scr[nb])

    @pl.when(e == 0)
    def _first():
        x = x_ref[0]
        h2 = _ln_stats(x) * (1.0 + sc_ref[0]) + sh_ref[0]
        h2t = h2.T
        h2s_scr[...] = h2t.astype(BF16)
        for nb in range(n_nb):
            blk = h2t[:, nb * width:(nb + 1) * width]
            h2g_scr[nb] = (blk * math.sqrt(0.5)).astype(BF16)
        ft_scr[...] = jnp.zeros(ft_scr.shape, F32)
        for h0 in range(0, PEER_HEADS, 2):
            qt = _dot(wqt_ref[h0 * D_KEY:(h0 + 2) * D_KEY, :], h2s_scr[...])
            for hp in range(4):
                h, p = h0 + hp // 2, hp % 2
                s = _dot(sk_ref[h, p], qt[hp * D_HALF:(hp + 1) * D_HALF, :].astype(BF16))
                for tt in range(n_tt):
                    s_scr[p, h, tt] = s[:, tt * LANES:(tt + 1) * LANES]

        cand_idx = cidx_ref[...]

        def put(h, tt, alpha, cnt1, beta, rank2):
            alpha_scr[h, tt] = alpha * math.sqrt(0.5)
            cnt_scr[h, tt] = cnt1
            beta_scr[h, tt] = beta.astype(BF16)
            rho_scr[h, tt] = rank2.astype(BF16)

        def topk_body(it, all_ok):
            h = it // n_tt
            tt = it % n_tt
            alpha, cnt1, beta, rank2, ok = _retrieve_by_threshold(s_scr[0, h, tt], s_scr[1, h, tt])
            put(h, tt, alpha, cnt1, beta, rank2)
            return jnp.minimum(all_ok, jnp.where(ok, 1.0, 0.0))

        all_ok = lax.fori_loop(0, PEER_HEADS * n_tt, topk_body,
                               jnp.ones((SUBLANES, LANES), F32))

        @pl.when(jnp.min(all_ok) < 0.5)
        def _():
            def exact_body(it, carry):
                h = it // n_tt
                tt = it % n_tt
                put(h, tt, *_retrieve(s_scr[0, h, tt], s_scr[1, h, tt], cand_idx))
                return carry

            lax.fori_loop(0, PEER_HEADS * n_tt, exact_body, 0)

        activations()

    @pl.when(jnp.logical_and(e > 0, e < n_chunks))
    def _mxu():
        accumulate()
        activations()

    @pl.when(e < n_chunks)
    def _gate():
        for ib in range(0, KEYS_PER_CHUNK, KEYS_PER_STEP):
            _gate_key_block(e * KEYS_PER_CHUNK + ib, ib, n_tt, per_nb, act_scr, ct_scr,
                            alpha_scr, cnt_scr, beta_scr, rho_scr)

    @pl.when(e == n_chunks)
    def _last():
        accumulate()
        for nb in range(n_nb):
            f = ft_scr[nb].T
            rows = slice(nb * width, (nb + 1) * width)
            g = g_ref[0] if g_ref.shape[1] == 1 else g_ref[0, rows, :]
            y = ALPHA * x_ref[0, rows, :] + g * f
            o_ref[0, rows, :] = _ln_stats(y) * lng_ref[...] + lnb_ref[...]


def _peer(x, sh, sc, g, wqt, sk, cand_idx, u, vt, lng, lnb, *, tile):
    B, S, D = x.shape
    nt = S // tile
    ne = N_EXPERTS // EXPERT_CHUNK
    n_tt = tile // LANES
    width = min(tile, MXU_COLS)
    n_nb = tile // width
    R = sh.shape[1]
    if R == 1:
        mod_spec = pl.BlockSpec((1, 1, D), lambda b, t, e: (b, 0, 0))
    else:
        mod_spec = pl.BlockSpec((1, tile, D), lambda b, t, e: (b, t, 0))
    kern = functools.partial(_peer_kernel, tile=tile, width=width)
    return pl.pallas_call(
        kern,
        grid=(B, nt, ne + 1),
        in_specs=[
            pl.BlockSpec((1, tile, D), lambda b, t, e: (b, t, 0)),
            mod_spec, mod_spec, mod_spec,
            pl.BlockSpec((PEER_HEADS * D_KEY, D), lambda b, t, e: (0, 0)),
            pl.BlockSpec((PEER_HEADS, 2, N_KEYS, D_HALF), lambda b, t, e: (0, 0, 0, 0)),
            pl.BlockSpec((N_CAND, LANES), lambda b, t, e: (0, 0)),
            pl.BlockSpec((EXPERT_CHUNK, D), lambda b, t, e: (jnp.minimum(e, ne - 1), 0)),
            pl.BlockSpec((D, EXPERT_CHUNK), lambda b, t, e: (0, jnp.clip(e - 1, 0, ne - 1))),
            pl.BlockSpec((1, D), lambda b, t, e: (0, 0)),
            pl.BlockSpec((1, D), lambda b, t, e: (0, 0)),
        ],
        out_specs=pl.BlockSpec((1, tile, D), lambda b, t, e: (b, t, 0)),
        out_shape=jax.ShapeDtypeStruct((B, S, D), F32),
        scratch_shapes=[
            pltpu.VMEM((D, tile), BF16),
            pltpu.VMEM((n_nb, D, width), BF16),
            pltpu.VMEM((n_nb, D, width), F32),
            pltpu.VMEM((2, PEER_HEADS, n_tt, N_KEYS, LANES), F32),
            pltpu.VMEM((PEER_HEADS, n_tt, N_KEYS, LANES), F32),
            pltpu.VMEM((PEER_HEADS, n_tt, N_KEYS, LANES), F32),
            pltpu.VMEM((PEER_HEADS, n_tt, N_KEYS, LANES), BF16),
            pltpu.VMEM((PEER_HEADS, n_tt, N_KEYS, LANES), BF16),
            pltpu.VMEM((n_nb, EXPERT_CHUNK, width), BF16),
            pltpu.VMEM((n_nb, EXPERT_CHUNK, width), BF16),
        ],
        compiler_params=pltpu.CompilerParams(
            dimension_semantics=("arbitrary", "arbitrary", "arbitrary"),
            vmem_limit_bytes=VMEM_LIMIT_BYTES),
        name="peer",
    )(x, sh, sc, g, wqt, sk, cand_idx, u, vt, lng, lnb)


def _expert_tables_kernel(u_ref, v_ref, ub_ref, vt_ref):
    ub_ref[...] = u_ref[0].astype(BF16)
    vt_ref[...] = v_ref[0].T.astype(BF16)


def _expert_tables(u_experts, v_experts, layer):
    _, E, D = u_experts.shape
    in_spec = pl.BlockSpec((1, TABLE_ROWS, D), lambda i: (layer, i, 0))
    return pl.pallas_call(
        _expert_tables_kernel,
        grid=(E // TABLE_ROWS,),
        in_specs=[in_spec, in_spec],
        out_specs=[pl.BlockSpec((TABLE_ROWS, D), lambda i: (i, 0)),
                   pl.BlockSpec((D, TABLE_ROWS), lambda i: (0, i))],
        out_shape=[jax.ShapeDtypeStruct((E, D), BF16), jax.ShapeDtypeStruct((D, E), BF16)],
        compiler_params=pltpu.CompilerParams(
            dimension_semantics=("arbitrary",), vmem_limit_bytes=VMEM_LIMIT_BYTES),
        name="expert_tables",
    )(u_experts, v_experts)


def _cand_index_table():
    idx = []
    for r, n in _CAND_SEGMENTS:
        idx.extend(r * PEER_TOPK + k for k in range(n))
    col = np.asarray(idx, np.float32)[:, None]
    return jnp.asarray(np.broadcast_to(col, (N_CAND, LANES)).copy())


def kernel(x_prompt, x_sample, state_conv, state_pool, c_prompt, c_sample, w_ada, b_ada, w_in, conv_w, pool_w, pool_scale, w_out, ln1_g, ln1_b, w_q, sub_keys, u_experts, v_experts, ln2_g, ln2_b):
    B, S, D = x_prompt.shape
    Bs = x_sample.shape[0]

    cand_idx = _cand_index_table()

    c_all = jnp.concatenate([c_prompt, c_sample], axis=0)
    mod = _ada_modulation(c_all, w_ada, b_ada)

    xp = x_prompt
    xs = x_sample.reshape(Bs, D)
    conv_p, pool_p, conv_s, pool_s = [], [], [], []
    for l in range(DEPTH):
        mp = [mod[l, :B, k * D:(k + 1) * D].reshape(B, 1, D) for k in range(6)]
        ms = [mod[l, B:, k * D:(k + 1) * D] for k in range(6)]
        row = lambda a: a[l].reshape(1, -1)
        w_in_l = w_in[l].astype(BF16)
        w_out_l = w_out[l].astype(BF16)
        pool_w_l = pool_w[l].astype(BF16)
        wqt_l = w_q[l].T.astype(BF16)
        sk_l = sub_keys[l].astype(BF16)
        u_l, vt_l = _expert_tables(u_experts, v_experts, l)

        xp, cst, pst = _mix_prompt(xp, mp[0], mp[1], mp[2], w_in_l, conv_w[l], pool_w_l,
                                   row(pool_scale), w_out_l, row(ln1_g), row(ln1_b))
        conv_p.append(cst)
        pool_p.append(pst)
        xp = _peer(xp, mp[3], mp[4], mp[5], wqt_l, sk_l, cand_idx, u_l, vt_l,
                   row(ln2_g), row(ln2_b), tile=PEER_TILE)

        xs, cst, pst = _mix_sample(
            xs, ms[0], ms[1], ms[2],
            state_conv[l].reshape(Bs, (CONV_W - 1) * D_CONV),
            state_pool[l].reshape(Bs, (POOL_MAX - 1) * D_POOL),
            w_in_l, conv_w[l], pool_w_l, row(pool_scale), w_out_l,
            row(ln1_g), row(ln1_b))
        conv_s.append(cst.reshape(Bs, CONV_W - 1, D_CONV))
        pool_s.append(pst.reshape(Bs, POOL_MAX - 1, D_POOL))
        xs = _peer(xs.reshape(1, Bs, D), ms[3].reshape(1, Bs, D), ms[4].reshape(1, Bs, D),
                   ms[5].reshape(1, Bs, D), wqt_l, sk_l, cand_idx, u_l, vt_l,
                   row(ln2_g), row(ln2_b), tile=Bs).reshape(Bs, D)

    return (xp, xs.reshape(Bs, 1, D), jnp.stack(conv_p), jnp.stack(pool_p),
            jnp.stack(conv_s), jnp.stack(pool_s))
```
